```python
import math
import jax, jax.numpy as jnp
from jax import lax
import numpy as np

D_MODEL = 1024
BATCH = 8
SEQ = 4096
DEPTH = 2

SSM_HEADS = 16
SSM_HEAD_DIM = 64
SSM_INNER = SSM_HEADS * SSM_HEAD_DIM
SSM_GROUPS = 4
SSM_HEADS_PER_GROUP = SSM_HEADS // SSM_GROUPS
SSM_STATE = 128
SSM_CONV = 4
SSM_CHUNK = 128
SSM_CONV_DIM = SSM_INNER + 2 * SSM_GROUPS * SSM_STATE

ATTN_HEAD_DIM = 64
ATTN_SLOTS = 8
ATTN_PATTERNS = ((128, 1), (512, 4), (2048, 16))
ATTN_GROUPS = len(ATTN_PATTERNS)
ATTN_Q_WIDTH = ATTN_GROUPS * ATTN_SLOTS * ATTN_HEAD_DIM
ATTN_KV_WIDTH = ATTN_SLOTS * ATTN_HEAD_DIM
ATTN_BLOCK = 128

SC_WIDTH = 512
SC_CONV = 3

N_BRANCH = 3
NORM_EPS = 1e-6

SPLITS = (
    SSM_INNER,
    SSM_CONV_DIM,
    SSM_HEADS,
    ATTN_Q_WIDTH,
    ATTN_KV_WIDTH,
    ATTN_KV_WIDTH,
    ATTN_KV_WIDTH,
    SC_WIDTH,
    SC_WIDTH,
    SC_WIDTH,
    SC_WIDTH,
    N_BRANCH * D_MODEL,
)
IN_WIDTH = sum(SPLITS)
SPLIT_OFFSETS = tuple(int(o) for o in np.cumsum(SPLITS)[:-1])

kernel_name = "hybrid_ssd_dilated_attn_shortconv"


def rms_norm(x, w):
    xf = x.astype(jnp.float32)
    y = xf * lax.rsqrt(jnp.mean(xf * xf, axis=-1, keepdims=True) + NORM_EPS)
    return (y * w.astype(jnp.float32)).astype(x.dtype)


def causal_depthwise_conv(x, w):
    k_width = w.shape[0]
    return lax.conv_general_dilated(
        x, w[:, None, :].astype(x.dtype), window_strides=(1,),
        padding=((k_width - 1, 0),), dimension_numbers=("NWC", "WIO", "NWC"),
        feature_group_count=x.shape[-1])


def ssd_scan(xdt, a, b_in, c_in):
    bsz, s = xdt.shape[:2]
    n_chunks = s // SSM_CHUNK

    def chunks(t):
        return jnp.moveaxis(t.reshape(bsz, n_chunks, SSM_CHUNK, *t.shape[2:]), 1, 0)

    causal = jnp.tril(jnp.ones((SSM_CHUNK, SSM_CHUNK), dtype=bool))[None, :, :, None, None]

    def step(state, inp):
        xc, ac, bc, cc = inp
        cs = jnp.cumsum(ac, axis=1)
        seg = cs[:, :, None] - cs[:, None, :]
        lmat = jnp.exp(jnp.where(causal, seg, -jnp.inf))
        cb = jnp.einsum("blgn,bsgn->blsg", cc, bc)
        y_diag = jnp.einsum("blsg,blsgj,bsgjp->blgjp", cb, lmat, xc)
        y_off = jnp.einsum("blgn,bgjpn,blgj->blgjp", cc, state, jnp.exp(cs))
        decay = jnp.exp(cs[:, -1:] - cs)
        new_state = (state * jnp.exp(cs[:, -1])[..., None, None]
                     + jnp.einsum("blgn,blgj,blgjp->bgjpn", bc, decay, xc))
        return new_state, y_diag + y_off

    state0 = jnp.zeros((bsz, SSM_GROUPS, SSM_HEADS_PER_GROUP, SSM_HEAD_DIM, SSM_STATE), jnp.float32)
    _, ys = lax.scan(step, state0, (chunks(xdt), chunks(a), chunks(b_in), chunks(c_in)))
    return jnp.moveaxis(ys, 0, 1).reshape(xdt.shape)


def ssd_branch(z, xbc, dt_raw, conv_w, conv_b, dt_bias, a_log, d_skip, norm_w):
    bsz, s, _ = xbc.shape
    xbc = jax.nn.silu(causal_depthwise_conv(xbc, conv_w) + conv_b.astype(xbc.dtype))
    gn = SSM_GROUPS * SSM_STATE
    xs = xbc[..., :SSM_INNER].astype(jnp.float32).reshape(
        bsz, s, SSM_GROUPS, SSM_HEADS_PER_GROUP, SSM_HEAD_DIM)
    b_in = xbc[..., SSM_INNER:SSM_INNER + gn].astype(jnp.float32).reshape(bsz, s, SSM_GROUPS, SSM_STATE)
    c_in = xbc[..., SSM_INNER + gn:].astype(jnp.float32).reshape(bsz, s, SSM_GROUPS, SSM_STATE)
    dt = jax.nn.softplus(dt_raw.astype(jnp.float32) + dt_bias.astype(jnp.float32))
    dt = dt.reshape(bsz, s, SSM_GROUPS, SSM_HEADS_PER_GROUP)
    a_neg = -jnp.exp(a_log.astype(jnp.float32)).reshape(SSM_GROUPS, SSM_HEADS_PER_GROUP)
    y = ssd_scan(xs * dt[..., None], dt * a_neg, b_in, c_in)
    y = y + xs * d_skip.astype(jnp.float32).reshape(SSM_GROUPS, SSM_HEADS_PER_GROUP, 1)
    yg = y.reshape(bsz, s, SSM_INNER) * jax.nn.silu(z.astype(jnp.float32))
    yg = yg.reshape(bsz, s, SSM_GROUPS, SSM_INNER // SSM_GROUPS)
    yg = yg * lax.rsqrt(jnp.mean(yg * yg, axis=-1, keepdims=True) + NORM_EPS)
    yg = yg.reshape(bsz, s, SSM_INNER) * norm_w.astype(jnp.float32)
    return yg.astype(z.dtype)


def dilated_window_attention(q, k, v, window, dilation):
    bsz, s, h, dh = q.shape
    n_back = window // dilation
    length = s // dilation
    n_blocks = -(-length // ATTN_BLOCK)
    padded = n_blocks * ATTN_BLOCK

    def residues(t):
        return t.reshape(bsz, length, dilation, h, dh).transpose(0, 2, 1, 3, 4)

    qr = jnp.pad(residues(q), ((0, 0), (0, 0), (0, padded - length), (0, 0), (0, 0)))
    qb = qr.reshape(bsz, dilation, n_blocks, ATTN_BLOCK, h, dh)

    def key_blocks(t):
        tr = jnp.pad(residues(t), ((0, 0), (0, 0), (ATTN_BLOCK, padded - length), (0, 0), (0, 0)))
        tr = tr.reshape(bsz, dilation, n_blocks + 1, ATTN_BLOCK, h, dh)
        return jnp.concatenate([tr[:, :, :-1], tr[:, :, 1:]], axis=3)

    kb = key_blocks(k)
    vb = key_blocks(v)
    q_idx = jnp.arange(ATTN_BLOCK)[:, None] + ATTN_BLOCK
    k_idx = jnp.arange(2 * ATTN_BLOCK)[None, :]
    diff = q_idx - k_idx
    band = (diff >= 0) & (diff <= n_back)
    key_pos = jnp.arange(n_blocks)[:, None] * ATTN_BLOCK - ATTN_BLOCK + k_idx
    valid = band[None] & (key_pos >= 0)[:, None, :]

    scores = jnp.einsum("brnqhd,brnkhd->brnhqk", qb, kb) * (ATTN_HEAD_DIM ** -0.5)
    scores = jnp.where(valid[None, None, :, None], scores, -jnp.inf)
    m = jnp.max(scores, axis=-1, keepdims=True)
    p = jnp.exp(scores - m)
    den = jnp.sum(p, axis=-1)
    o = jnp.einsum("brnhqk,brnkhd->brnqhd", p, vb) / jnp.moveaxis(den, -1, -2)[..., None]
    lse = jnp.moveaxis(m[..., 0] + jnp.log(den), -1, -2)

    o = o.reshape(bsz, dilation, padded, h, dh)[:, :, :length]
    o = o.transpose(0, 2, 1, 3, 4).reshape(bsz, s, h, dh)
    lse = lse.reshape(bsz, dilation, padded, h)[:, :, :length]
    lse = lse.transpose(0, 2, 1, 3).reshape(bsz, s, h)
    return o, lse


def attention_branch(q, k, v):
    bsz, s, _ = q.shape
    qf = q.astype(jnp.float32).reshape(bsz, s, ATTN_GROUPS, ATTN_SLOTS, ATTN_HEAD_DIM)
    kf = k.astype(jnp.float32).reshape(bsz, s, ATTN_SLOTS, ATTN_HEAD_DIM)
    vf = v.astype(jnp.float32).reshape(bsz, s, ATTN_SLOTS, ATTN_HEAD_DIM)
    outs, lses = [], []
    for g, (window, dilation) in enumerate(ATTN_PATTERNS):
        o, lse = dilated_window_attention(qf[:, :, g], kf, vf, window, dilation)
        outs.append(o)
        lses.append(lse)
    weights = jax.nn.softmax(jnp.stack(lses, axis=0), axis=0)
    out = jnp.sum(weights[..., None] * jnp.stack(outs, axis=0), axis=0)
    return out.reshape(bsz, s, ATTN_KV_WIDTH).astype(q.dtype)


def hybrid_layer(x, norm_pre, norm_post, w_in, ssm_conv_w, ssm_conv_b, dt_bias, a_log,
                 d_skip, ssm_norm, sc_conv_w, p_ssm, p_attn, p_sc, w_out):
    bsz, s, _ = x.shape
    h = rms_norm(x, norm_pre)
    proj = h @ w_in.astype(h.dtype)
    (z, xbc, dt_raw, q, k, v, g_attn, u, b_sc, c_sc, g_sc, merge) = jnp.split(
        proj, SPLIT_OFFSETS, axis=-1)

    y_ssm = ssd_branch(z, xbc, dt_raw, ssm_conv_w, ssm_conv_b, dt_bias, a_log, d_skip, ssm_norm)
    y_attn = attention_branch(q, k, v) * jax.nn.silu(g_attn)
    y_sc = b_sc * causal_depthwise_conv(c_sc * u, sc_conv_w) * jax.nn.silu(g_sc)

    gates = jax.nn.sigmoid(merge.astype(jnp.float32)).reshape(bsz, s, N_BRANCH, D_MODEL)
    merged = (gates[:, :, 0] * (y_ssm @ p_ssm.astype(y_ssm.dtype)).astype(jnp.float32)
              + gates[:, :, 1] * (y_attn @ p_attn.astype(y_attn.dtype)).astype(jnp.float32)
              + gates[:, :, 2] * (y_sc @ p_sc.astype(y_sc.dtype)).astype(jnp.float32))
    out = merged.astype(x.dtype) @ w_out.astype(x.dtype)
    return x + rms_norm(out, norm_post)


def setup_inputs(seed: int = 0) -> dict:
    key = jax.random.key(seed)
    ks = jax.random.split(key, 16)

    def nrm(k, shape, scale):
        return jax.random.normal(k, shape, jnp.float32) * scale

    x = nrm(ks[0], (BATCH, SEQ, D_MODEL), 1.0)
    norm_pre = 1.0 + nrm(ks[1], (DEPTH, D_MODEL), 0.05)
    norm_post = 1.0 + nrm(ks[2], (DEPTH, D_MODEL), 0.05)
    w_in = nrm(ks[3], (DEPTH, D_MODEL, IN_WIDTH), D_MODEL ** -0.5)
    ssm_conv_w = nrm(ks[4], (DEPTH, SSM_CONV, SSM_CONV_DIM), SSM_CONV ** -0.5)
    ssm_conv_b = nrm(ks[5], (DEPTH, SSM_CONV_DIM), 0.01)
    dt0 = jnp.exp(jax.random.uniform(ks[6], (DEPTH, SSM_HEADS), jnp.float32,
                                     math.log(1e-3), math.log(1e-1)))
    dt_bias = dt0 + jnp.log(-jnp.expm1(-dt0))
    a_log = jnp.log(jax.random.uniform(ks[7], (DEPTH, SSM_HEADS), jnp.float32, 1.0, 16.0))
    d_skip = 1.0 + nrm(ks[8], (DEPTH, SSM_HEADS), 0.05)
    ssm_norm = 1.0 + nrm(ks[9], (DEPTH, SSM_INNER), 0.05)
    sc_conv_w = nrm(ks[10], (DEPTH, SC_CONV, SC_WIDTH), SC_CONV ** -0.5)
    p_ssm = nrm(ks[11], (DEPTH, SSM_INNER, D_MODEL), SSM_INNER ** -0.5)
    p_attn = nrm(ks[12], (DEPTH, ATTN_KV_WIDTH, D_MODEL), ATTN_KV_WIDTH ** -0.5)
    p_sc = nrm(ks[13], (DEPTH, SC_WIDTH, D_MODEL), SC_WIDTH ** -0.5)
    w_out = nrm(ks[14], (DEPTH, D_MODEL, D_MODEL), D_MODEL ** -0.5)
    return {"x": x, "norm_pre": norm_pre, "norm_post": norm_post, "w_in": w_in,
            "ssm_conv_w": ssm_conv_w, "ssm_conv_b": ssm_conv_b, "dt_bias": dt_bias,
            "a_log": a_log, "d_skip": d_skip, "ssm_norm": ssm_norm, "sc_conv_w": sc_conv_w,
            "p_ssm": p_ssm, "p_attn": p_attn, "p_sc": p_sc, "w_out": w_out}


def reference(x, norm_pre, norm_post, w_in, ssm_conv_w, ssm_conv_b, dt_bias, a_log,
              d_skip, ssm_norm, sc_conv_w, p_ssm, p_attn, p_sc, w_out):
    for i in range(DEPTH):
        x = hybrid_layer(x, norm_pre[i], norm_post[i], w_in[i], ssm_conv_w[i], ssm_conv_b[i],
                         dt_bias[i], a_log[i], d_skip[i], ssm_norm[i], sc_conv_w[i],
                         p_ssm[i], p_attn[i], p_sc[i], w_out[i])
    return x
```

```python
import functools

import jax
import jax.numpy as jnp
import numpy as np
from jax import lax
from jax.experimental import pallas as pl
from jax.experimental.pallas import tpu as pltpu

F32 = jnp.float32
BF16 = jnp.bfloat16

D_MODEL = 1024
SSM_HEADS = 16
SSM_HEAD_DIM = 64
SSM_INNER = SSM_HEADS * SSM_HEAD_DIM
SSM_GROUPS = 4
SSM_HEADS_PER_GROUP = SSM_HEADS // SSM_GROUPS
SSM_STATE = 128
SSM_CONV = 4
SSM_CHUNK = 128
SSM_CONV_DIM = SSM_INNER + 2 * SSM_GROUPS * SSM_STATE
ATTN_HEAD_DIM = 64
ATTN_SLOTS = 8
ATTN_PATTERNS = ((128, 1), (512, 4), (2048, 16))
ATTN_KV_WIDTH = ATTN_SLOTS * ATTN_HEAD_DIM
ATTN_BLOCK = 128
SC_WIDTH = 512
SC_CONV = 3
NORM_EPS = 1e-6

LANES = 128
SUBLANES = 8
VMEM_LIMIT = 48 * 1024 * 1024

COL_XBC = 0
COL_Z = COL_XBC + SSM_CONV_DIM
COL_Q = COL_Z + SSM_INNER
COL_K = COL_Q + 3 * ATTN_KV_WIDTH
COL_V = COL_K + ATTN_KV_WIDTH
COL_GATTN = COL_V + ATTN_KV_WIDTH
COL_SC = COL_GATTN + ATTN_KV_WIDTH
COL_MERGE = COL_SC + 4 * SC_WIDTH
PROJ_WIDTH = COL_MERGE + 3 * D_MODEL


def _sigmoid(x):
    return 1.0 / (1.0 + jnp.exp(-x))


def _silu(x):
    return x * _sigmoid(x)


def _split_hi_lo(v):
    hi = v.astype(BF16)
    lo = (v - hi.astype(F32)).astype(BF16)
    return jnp.concatenate([hi, lo], axis=1)


def _inproj_body(x_ref, nw_ref, w_ref, wdt_ref, o_ref, dt_ref, h_scr):
    @pl.when(pl.program_id(1) == 0)
    def _():
        x = x_ref[...]
        ms = jnp.mean(x * x, axis=-1, keepdims=True)
        h = (x * lax.rsqrt(ms + NORM_EPS) * nw_ref[...]).astype(BF16)
        h_scr[...] = h
        dt_ref[...] = jnp.dot(h, wdt_ref[...], preferred_element_type=F32)

    o_ref[...] = jnp.dot(h_scr[...], w_ref[...],
                         preferred_element_type=F32).astype(BF16)


def _in_proj(x2d, norm_w, w_perm, w_dt, tm=1024, tn=1024):
    m = x2d.shape[0]
    return pl.pallas_call(
        _inproj_body,
        grid=(m // tm, PROJ_WIDTH // tn),
        in_specs=[
            pl.BlockSpec((tm, D_MODEL), lambda i, j: (i, 0)),
            pl.BlockSpec((1, D_MODEL), lambda i, j: (0, 0)),
            pl.BlockSpec((D_MODEL, tn), lambda i, j: (0, j)),
            pl.BlockSpec((D_MODEL, LANES), lambda i, j: (0, 0)),
        ],
        out_specs=[
            pl.BlockSpec((tm, tn), lambda i, j: (i, j)),
            pl.BlockSpec((tm, LANES), lambda i, j: (i, 0)),
        ],
        out_shape=[
            jax.ShapeDtypeStruct((m, PROJ_WIDTH), BF16),
            jax.ShapeDtypeStruct((m, LANES), F32),
        ],
        scratch_shapes=[pltpu.VMEM((tm, D_MODEL), BF16)],
        compiler_params=pltpu.CompilerParams(
            dimension_semantics=("arbitrary", "arbitrary"),
            vmem_limit_bytes=VMEM_LIMIT),
        name="in_proj",
    )(x2d, norm_w, w_perm, w_dt)


def _ssd_body(xbc_ref, z_ref, dt_ref, cw_ref, cb_ref, dtb_ref, aneg_ref,
              dskip_ref, nw_ref, expand_ref, tri_ref, o_ref,
              ext_scr, xs_scr, b_scr, c_scr, state_scr):
    q = SSM_CHUNK
    halo = SUBLANES
    first = pl.program_id(1) == 0

    @pl.when(first)
    def _():
        ext_scr[0:halo, :] = jnp.zeros((halo, SSM_CONV_DIM), F32)
        state_scr[...] = jnp.zeros_like(state_scr)

    @pl.when(jnp.logical_not(first))
    def _():
        ext_scr[0:halo, :] = ext_scr[q:q + halo, :]

    ext_scr[halo:halo + q, :] = xbc_ref[0].astype(F32)

    piece = 512
    for p in range(SSM_CONV_DIM // piece):
        cols = slice(p * piece, (p + 1) * piece)
        acc = jnp.broadcast_to(cb_ref[:, cols], (q, piece))
        for k in range(SSM_CONV):
            lo = halo - (SSM_CONV - 1) + k
            acc = acc + cw_ref[k:k + 1, cols] * ext_scr[lo:lo + q, cols]
        act = _silu(acc)
        if p < 2:
            xs_scr[:, cols] = act
        elif p == 2:
            b_scr[...] = act.astype(BF16)
        else:
            c_scr[...] = act.astype(BF16)

    dt_in = dt_ref[0] + dtb_ref[...]
    dt = jnp.maximum(dt_in, 0.0) + jnp.log(1.0 + jnp.exp(-jnp.abs(dt_in)))
    a = dt * aneg_ref[...]
    cs = jnp.dot(tri_ref[...], a, precision=lax.Precision.HIGHEST,
                 preferred_element_type=F32)
    cs_t = cs.T
    ecs = jnp.exp(cs)
    dec = jnp.exp(cs[q - 1:q, :] - cs)
    stacked = jnp.concatenate(
        [_split_hi_lo(dt), _split_hi_lo(ecs), _split_hi_lo(dec)], axis=0)
    expanded = jnp.dot(stacked, expand_ref[...], preferred_element_type=F32)
    dt_x = expanded[0:q]
    ecs_x = expanded[q:2 * q]
    dec_x = expanded[2 * q:3 * q]

    xs = xs_scr[...]
    xdt = xs * dt_x
    xdt_b = xdt.astype(BF16)
    xdec_b = (xdt * dec_x).astype(BF16)

    row = lax.broadcasted_iota(jnp.int32, (q, q), 0)
    col = lax.broadcasted_iota(jnp.int32, (q, q), 1)
    causal = row >= col

    gw = SSM_HEADS_PER_GROUP * SSM_HEAD_DIM
    for g in range(SSM_GROUPS):
        gcols = slice(g * gw, (g + 1) * gw)
        ncols = slice(g * SSM_STATE, (g + 1) * SSM_STATE)
        bg = b_scr[:, ncols]
        cg = c_scr[:, ncols]
        cb = lax.dot_general(cg, bg, (((1,), (1,)), ((), ())),
                             preferred_element_type=F32)
        st = state_scr[:, gcols]
        y_off = jnp.dot(cg, st.astype(BF16),
                        preferred_element_type=F32) * ecs_x[:, gcols]
        y_parts = []
        for j in range(SSM_HEADS_PER_GROUP):
            h = g * SSM_HEADS_PER_GROUP + j
            seg = cs[:, h:h + 1] - cs_t[h:h + 1, :]
            lmat = jnp.exp(jnp.where(causal, seg, -jnp.inf))
            mh = (cb * lmat).astype(BF16)
            y_parts.append(jnp.dot(
                mh, xdt_b[:, h * SSM_HEAD_DIM:(h + 1) * SSM_HEAD_DIM],
                preferred_element_type=F32))
        y_g = jnp.concatenate(y_parts, axis=1) + y_off
        upd = lax.dot_general(bg, xdec_b[:, gcols], (((0,), (0,)), ((), ())),
                              preferred_element_type=F32)
        state_scr[:, gcols] = st * ecs_x[q - 1:q, gcols] + upd

        y_g = y_g + xs[:, gcols] * dskip_ref[:, gcols]
        yg = y_g * _silu(z_ref[0, :, gcols].astype(F32))
        ms = jnp.mean(yg * yg, axis=-1, keepdims=True)
        yg = yg * lax.rsqrt(ms + NORM_EPS) * nw_ref[:, gcols]
        o_ref[0, :, gcols] = yg.astype(BF16)


def _ssd(proj3, dt3, cw, cb, dtb, aneg, dskip_x, nw, expand_mat, tri):
    bsz, s, _ = proj3.shape
    q = SSM_CHUNK
    const = lambda b, c: (0, 0)
    return pl.pallas_call(
        _ssd_body,
        grid=(bsz, s // q),
        in_specs=[
            pl.BlockSpec((1, q, SSM_CONV_DIM), lambda b, c: (b, c, COL_XBC // SSM_CONV_DIM)),
            pl.BlockSpec((1, q, SSM_INNER), lambda b, c: (b, c, COL_Z // SSM_INNER)),
            pl.BlockSpec((1, q, LANES), lambda b, c: (b, c, 0)),
            pl.BlockSpec((SSM_CONV, SSM_CONV_DIM), const),
            pl.BlockSpec((1, SSM_CONV_DIM), const),
            pl.BlockSpec((1, LANES), const),
            pl.BlockSpec((1, LANES), const),
            pl.BlockSpec((1, SSM_INNER), const),
            pl.BlockSpec((1, SSM_INNER), const),
            pl.BlockSpec((2 * LANES, SSM_INNER), const),
            pl.BlockSpec((q, q), const),
        ],
        out_specs=pl.BlockSpec((1, q, SSM_INNER), lambda b, c: (b, c, 0)),
        out_shape=jax.ShapeDtypeStruct((bsz, s, SSM_INNER), BF16),
        scratch_shapes=[
            pltpu.VMEM((q + 2 * SUBLANES, SSM_CONV_DIM), F32),
            pltpu.VMEM((q, SSM_INNER), F32),
            pltpu.VMEM((q, SSM_GROUPS * SSM_STATE), BF16),
            pltpu.VMEM((q, SSM_GROUPS * SSM_STATE), BF16),
            pltpu.VMEM((SSM_STATE, SSM_INNER), F32),
        ],
        compiler_params=pltpu.CompilerParams(
            dimension_semantics=("arbitrary", "arbitrary"),
            vmem_limit_bytes=VMEM_LIMIT),
        name="ssd",
    )(proj3, proj3, dt3, cw, cb, dtb, aneg, dskip_x, nw, expand_mat, tri)


def _attn_body(q_ref, kp_ref, kc_ref, vp_ref, vc_ref, o_ref, lse_ref, *, n_back):
    blk = ATTN_BLOCK
    n = pl.program_id(1)
    qv = q_ref[0] * jnp.asarray(ATTN_HEAD_DIM ** -0.5, BF16)
    k2 = jnp.concatenate([kp_ref[0], kc_ref[0]], axis=0)
    v2 = jnp.concatenate([vp_ref[0], vc_ref[0]], axis=0)
    qi = lax.broadcasted_iota(jnp.int32, (blk, 2 * blk), 0) + blk
    kj = lax.broadcasted_iota(jnp.int32, (blk, 2 * blk), 1)
    diff = qi - kj
    valid = (diff >= 0) & (diff <= n_back) & ((kj >= blk) | (n > 0))
    ones = jnp.ones((2 * blk, ATTN_HEAD_DIM), BF16)
    lane = lax.broadcasted_iota(jnp.int32, (blk, LANES), 1)
    lse_mat = jnp.zeros((blk, LANES), F32)
    outs = []
    for h in range(ATTN_SLOTS):
        hc = slice(h * ATTN_HEAD_DIM, (h + 1) * ATTN_HEAD_DIM)
        s = lax.dot_general(qv[:, hc], k2[:, hc], (((1,), (1,)), ((), ())),
                            preferred_element_type=F32)
        s = jnp.where(valid, s, -jnp.inf)
        mx = jnp.max(s, axis=-1, keepdims=True)
        p = jnp.exp(s - mx).astype(BF16)
        vext = jnp.concatenate([v2[:, hc], ones], axis=1)
        r = jnp.dot(p, vext, preferred_element_type=F32)
        den = r[:, ATTN_HEAD_DIM:]
        outs.append(r[:, :ATTN_HEAD_DIM] / den)
        lse_h = mx + jnp.log(den[:, 0:1])
        lse_mat = jnp.where(lane == h, lse_h, lse_mat)
    o_ref[0] = jnp.concatenate(outs, axis=1).astype(BF16)
    lse_ref[0] = lse_mat


def _attention(qkv, q_col, k_col, v_col, n_back):
    r, length, _ = qkv.shape
    blk = ATTN_BLOCK
    w = ATTN_KV_WIDTH
    cur = lambda col: (lambda b, n: (b, n, col))
    prev = lambda col: (lambda b, n: (b, jnp.maximum(n - 1, 0), col))
    return pl.pallas_call(
        functools.partial(_attn_body, n_back=n_back),
        grid=(r, length // blk),
        in_specs=[
            pl.BlockSpec((1, blk, w), cur(q_col)),
            pl.BlockSpec((1, blk, w), prev(k_col)),
            pl.BlockSpec((1, blk, w), cur(k_col)),
            pl.BlockSpec((1, blk, w), prev(v_col)),
            pl.BlockSpec((1, blk, w), cur(v_col)),
        ],
        out_specs=[
            pl.BlockSpec((1, blk, w), lambda b, n: (b, n, 0)),
            pl.BlockSpec((1, blk, LANES), lambda b, n: (b, n, 0)),
        ],
        out_shape=[
            jax.ShapeDtypeStruct((r, length, w), BF16),
            jax.ShapeDtypeStruct((r, length, LANES), F32),
        ],
        compiler_params=pltpu.CompilerParams(
            dimension_semantics=("arbitrary", "arbitrary"),
            vmem_limit_bytes=VMEM_LIMIT),
        name="dilated_attention",
    )(qkv, qkv, qkv, qkv, qkv)


def _merge_body(x_ref, sc_ref, gat_ref, m0_ref, m1_ref, m2_ref, yssm_ref,
                o1_ref, o2_ref, o3_ref, l1_ref, l2_ref, l3_ref,
                scw_ref, pssm_ref, pattn_ref, psc_ref, wout_ref, nw_ref,
                expand_ref, out_ref, ext_scr, *, tiles_per_seq):
    tm = x_ref.shape[0]
    halo = SUBLANES
    first = (pl.program_id(0) % tiles_per_seq) == 0

    @pl.when(first)
    def _():
        ext_scr[0:halo, :] = jnp.zeros((halo, SC_WIDTH), F32)

    @pl.when(jnp.logical_not(first))
    def _():
        ext_scr[0:halo, :] = ext_scr[tm:tm + halo, :]

    u = sc_ref[:, 0:SC_WIDTH].astype(F32)
    b_sc = sc_ref[:, SC_WIDTH:2 * SC_WIDTH].astype(F32)
    c_sc = sc_ref[:, 2 * SC_WIDTH:3 * SC_WIDTH].astype(F32)
    g_sc = sc_ref[:, 3 * SC_WIDTH:4 * SC_WIDTH].astype(F32)
    ext_scr[halo:halo + tm, :] = c_sc * u
    conv = jnp.zeros((tm, SC_WIDTH), F32)
    for k in range(SC_CONV):
        lo = halo - (SC_CONV - 1) + k
        conv = conv + scw_ref[k:k + 1, :] * ext_scr[lo:lo + tm, :]
    y_sc = (b_sc * conv * _silu(g_sc)).astype(BF16)

    l1, l2, l3 = l1_ref[...], l2_ref[...], l3_ref[...]
    lm = jnp.maximum(jnp.maximum(l1, l2), l3)
    e1, e2, e3 = jnp.exp(l1 - lm), jnp.exp(l2 - lm), jnp.exp(l3 - lm)
    inv = 1.0 / (e1 + e2 + e3)
    y_attn = jnp.zeros((tm, ATTN_KV_WIDTH), F32)
    for e, o_ref in ((e1, o1_ref), (e2, o2_ref), (e3, o3_ref)):
        w_x = jnp.dot(_split_hi_lo(e * inv), expand_ref[...],
                      preferred_element_type=F32)
        y_attn = y_attn + w_x * o_ref[...].astype(F32)
    y_attn = (y_attn * _silu(gat_ref[...].astype(F32))).astype(BF16)

    merged = (
        _sigmoid(m0_ref[...].astype(F32))
        * jnp.dot(yssm_ref[...], pssm_ref[...], preferred_element_type=F32)
        + _sigmoid(m1_ref[...].astype(F32))
        * jnp.dot(y_attn, pattn_ref[...], preferred_element_type=F32)
        + _sigmoid(m2_ref[...].astype(F32))
        * jnp.dot(y_sc, psc_ref[...], preferred_element_type=F32))
    out = jnp.dot(merged.astype(BF16), wout_ref[...], preferred_element_type=F32)
    ms = jnp.mean(out * out, axis=-1, keepdims=True)
    out_ref[...] = x_ref[...] + out * lax.rsqrt(ms + NORM_EPS) * nw_ref[...]


def _merge(x2d, proj, y_ssm, o_list, lse_list, scw, p_ssm, p_attn, p_sc, w_out,
           nw, expand_mat, seq, tm=512):
    m = x2d.shape[0]
    row = lambda c: (lambda i: (i, c))
    const = lambda i: (0, 0)
    wa = ATTN_KV_WIDTH
    return pl.pallas_call(
        functools.partial(_merge_body, tiles_per_seq=seq // tm),
        grid=(m // tm,),
        in_specs=[
            pl.BlockSpec((tm, D_MODEL), row(0)),
            pl.BlockSpec((tm, 4 * SC_WIDTH), row(COL_SC // (4 * SC_WIDTH))),
            pl.BlockSpec((tm, wa), row(COL_GATTN // wa)),
            pl.BlockSpec((tm, D_MODEL), row(COL_MERGE // D_MODEL)),
            pl.BlockSpec((tm, D_MODEL), row(COL_MERGE // D_MODEL + 1)),
            pl.BlockSpec((tm, D_MODEL), row(COL_MERGE // D_MODEL + 2)),
            pl.BlockSpec((tm, SSM_INNER), row(0)),
            pl.BlockSpec((tm, wa), row(0)),
            pl.BlockSpec((tm, wa), row(0)),
            pl.BlockSpec((tm, wa), row(0)),
            pl.BlockSpec((tm, LANES), row(0)),
            pl.BlockSpec((tm, LANES), row(0)),
            pl.BlockSpec((tm, LANES), row(0)),
            pl.BlockSpec((SC_CONV, SC_WIDTH), const),
            pl.BlockSpec((SSM_INNER, D_MODEL), const),
            pl.BlockSpec((wa, D_MODEL), const),
            pl.BlockSpec((SC_WIDTH, D_MODEL), const),
            pl.BlockSpec((D_MODEL, D_MODEL), const),
            pl.BlockSpec((1, D_MODEL), const),
            pl.BlockSpec((2 * LANES, wa), const),
        ],
        out_specs=pl.BlockSpec((tm, D_MODEL), row(0)),
        out_shape=jax.ShapeDtypeStruct((m, D_MODEL), F32),
        scratch_shapes=[pltpu.VMEM((tm + 2 * SUBLANES, SC_WIDTH), F32)],
        compiler_params=pltpu.CompilerParams(
            dimension_semantics=("arbitrary",),
            vmem_limit_bytes=VMEM_LIMIT),
        name="merge_out",
    )(x2d, proj, proj, proj, proj, proj, y_ssm, *o_list, *lse_list,
      scw, p_ssm, p_attn, p_sc, w_out, nw, expand_mat)


def _expand_matrix(n_heads, width):
    m = np.zeros((2 * LANES, n_heads * width), np.float32)
    for h in range(n_heads):
        m[h, h * width:(h + 1) * width] = 1.0
        m[LANES + h, h * width:(h + 1) * width] = 1.0
    return jnp.asarray(m, BF16)


def _pad_lanes(v):
    return jnp.zeros((1, LANES), F32).at[0, :v.shape[0]].set(v.astype(F32))


def _permute_w_in(w):
    offs = np.cumsum([0, SSM_INNER, SSM_CONV_DIM, SSM_HEADS, 3 * ATTN_KV_WIDTH,
                      ATTN_KV_WIDTH, ATTN_KV_WIDTH, ATTN_KV_WIDTH,
                      4 * SC_WIDTH, 3 * D_MODEL])
    z, xbc, dt, q, k, v, gat, sc, mg = [w[:, offs[i]:offs[i + 1]] for i in range(9)]
    w_perm = jnp.concatenate([xbc, z, q, k, v, gat, sc, mg], axis=1).astype(BF16)
    w_dt = jnp.zeros((D_MODEL, LANES), F32).at[:, :SSM_HEADS].set(dt).astype(BF16)
    return w_perm, w_dt


def _layer(x2d, bsz, seq, norm_pre, norm_post, w_in, ssm_conv_w, ssm_conv_b, dt_bias,
           a_log, d_skip, ssm_norm, sc_conv_w, p_ssm, p_attn, p_sc, w_out,
           expand_ssm, expand_attn, tri):
    w_perm, w_dt = _permute_w_in(w_in)
    proj, dt_raw = _in_proj(x2d, norm_pre.reshape(1, D_MODEL), w_perm, w_dt)
    proj3 = proj.reshape(bsz, seq, PROJ_WIDTH)
    dt3 = dt_raw.reshape(bsz, seq, LANES)

    y_ssm = _ssd(
        proj3, dt3, ssm_conv_w, ssm_conv_b.reshape(1, SSM_CONV_DIM),
        _pad_lanes(dt_bias), _pad_lanes(-jnp.exp(a_log.astype(F32))),
        jnp.repeat(d_skip.astype(F32), SSM_HEAD_DIM).reshape(1, SSM_INNER),
        ssm_norm.reshape(1, SSM_INNER), expand_ssm, tri)

    wa = ATTN_KV_WIDTH
    o_list, lse_list = [], []
    for g, (window, dil) in enumerate(ATTN_PATTERNS):
        n_back = window // dil
        if dil == 1:
            o, lse = _attention(proj3, COL_Q // wa + g, COL_K // wa, COL_V // wa, n_back)
        else:
            length = seq // dil
            qg = proj3[:, :, COL_Q + g * wa:COL_Q + (g + 1) * wa]
            kv = proj3[:, :, COL_K:COL_K + 2 * wa]
            qkv = jnp.concatenate([qg, kv], axis=-1)
            qkv = qkv.reshape(bsz, length, dil, 3 * wa).transpose(0, 2, 1, 3)
            qkv = qkv.reshape(bsz * dil, length, 3 * wa)
            o, lse = _attention(qkv, 0, 1, 2, n_back)
            o = o.reshape(bsz, dil, length, wa).transpose(0, 2, 1, 3)
            lse = lse.reshape(bsz, dil, length, LANES).transpose(0, 2, 1, 3)
        o_list.append(o.reshape(bsz * seq, wa))
        lse_list.append(lse.reshape(bsz * seq, LANES))

    return _merge(x2d, proj, y_ssm.reshape(bsz * seq, SSM_INNER), o_list, lse_list,
                  sc_conv_w, p_ssm.astype(BF16), p_attn.astype(BF16),
                  p_sc.astype(BF16), w_out.astype(BF16),
                  norm_post.reshape(1, D_MODEL), expand_attn, seq)


def kernel(x, norm_pre, norm_post, w_in, ssm_conv_w, ssm_conv_b, dt_bias, a_log,
           d_skip, ssm_norm, sc_conv_w, p_ssm, p_attn, p_sc, w_out):
    bsz, seq, _ = x.shape
    expand_ssm = _expand_matrix(SSM_HEADS, SSM_HEAD_DIM)
    expand_attn = _expand_matrix(ATTN_SLOTS, ATTN_HEAD_DIM)
    tri = jnp.asarray(np.tril(np.ones((SSM_CHUNK, SSM_CHUNK), np.float32)))
    x2d = x.reshape(bsz * seq, D_MODEL)
    for i in range(norm_pre.shape[0]):
        x2d = _layer(x2d, bsz, seq, norm_pre[i], norm_post[i], w_in[i], ssm_conv_w[i],
                     ssm_conv_b[i], dt_bias[i], a_log[i], d_skip[i], ssm_norm[i],
                     sc_conv_w[i], p_ssm[i], p_attn[i], p_sc[i], w_out[i],
                     expand_ssm, expand_attn, tri)
    return x2d.reshape(bsz, seq, D_MODEL)
```

```python
import functools

import jax
import jax.numpy as jnp
import numpy as np
from jax import lax
from jax.experimental import pallas as pl
from jax.experimental.pallas import tpu as pltpu

F32 = jnp.float32
BF16 = jnp.bfloat16

D_MODEL = 1024
SSM_HEADS = 16
SSM_HEAD_DIM = 64
SSM_INNER = SSM_HEADS * SSM_HEAD_DIM
SSM_GROUPS = 4
SSM_HEADS_PER_GROUP = SSM_HEADS // SSM_GROUPS
SSM_STATE = 128
SSM_CONV = 4
SSM_CHUNK = 128
SSM_CONV_DIM = SSM_INNER + 2 * SSM_GROUPS * SSM_STATE
ATTN_HEAD_DIM = 64
ATTN_SLOTS = 8
ATTN_PATTERNS = ((128, 1), (512, 4), (2048, 16))
ATTN_KV_WIDTH = ATTN_SLOTS * ATTN_HEAD_DIM
ATTN_BLOCK = 128
SC_WIDTH = 512
SC_CONV = 3
ATTN_UNROLL = 8
NORM_EPS = 1e-6

LANES = 128
SUBLANES = 8
VMEM_LIMIT = 48 * 1024 * 1024

COL_XBC = 0
COL_SC = COL_XBC + SSM_CONV_DIM
COL_Z = COL_SC + 4 * SC_WIDTH
COL_MERGE = COL_Z + SSM_INNER
PROJ_WIDTH = COL_MERGE + 3 * D_MODEL

PAIRS = ATTN_KV_WIDTH // LANES
SLAB_Q = 0
SLAB_K = SLAB_Q + len(ATTN_PATTERNS) * PAIRS
SLAB_V = SLAB_K + PAIRS
SLAB_GATE = SLAB_V + PAIRS
N_SLABS = SLAB_GATE + PAIRS
QKV_WIDTH = N_SLABS * LANES


def _sigmoid(x):
    return 1.0 / (1.0 + jnp.exp(-x))


def _silu(x):
    return x * _sigmoid(x)


def _split_hi_lo(v):
    hi = v.astype(BF16)
    lo = (v - hi.astype(F32)).astype(BF16)
    return jnp.concatenate([hi, lo], axis=1)


def _normed(x_ref, nw_ref):
    x = x_ref[...]
    ms = jnp.mean(x * x, axis=-1, keepdims=True)
    return (x * lax.rsqrt(ms + NORM_EPS) * nw_ref[...]).astype(BF16)


def _qkvproj_body(x_ref, nw_ref, w_ref, o_ref, h_scr):
    @pl.when(pl.program_id(1) == 0)
    def _():
        h_scr[...] = _normed(x_ref, nw_ref)

    res = jnp.dot(h_scr[...], w_ref[...], preferred_element_type=F32)
    for s in range(o_ref.shape[0]):
        o_ref[s] = res[:, s * LANES:(s + 1) * LANES]


def _qkv_proj(x2d, norm_w, w_qkv, tm=1024, tn=1024):
    m = x2d.shape[0]
    return pl.pallas_call(
        _qkvproj_body,
        grid=(m // tm, QKV_WIDTH // tn),
        in_specs=[
            pl.BlockSpec((tm, D_MODEL), lambda i, j: (i, 0)),
            pl.BlockSpec((1, D_MODEL), lambda i, j: (0, 0)),
            pl.BlockSpec((D_MODEL, tn), lambda i, j: (0, j)),
        ],
        out_specs=pl.BlockSpec((tn // LANES, tm, LANES), lambda i, j: (j, i, 0)),
        out_shape=jax.ShapeDtypeStruct((N_SLABS, m, LANES), F32),
        scratch_shapes=[pltpu.VMEM((tm, D_MODEL), BF16)],
        compiler_params=pltpu.CompilerParams(
            dimension_semantics=("arbitrary", "arbitrary"),
            vmem_limit_bytes=VMEM_LIMIT),
        name="qkv_proj",
    )(x2d, norm_w, w_qkv)


def _inproj_body(x_ref, nw_ref, w_ref, wdt_ref, o_ref, dt_ref, h_scr):
    @pl.when(pl.program_id(1) == 0)
    def _():
        h = _normed(x_ref, nw_ref)
        h_scr[...] = h
        dt_ref[...] = jnp.dot(h, wdt_ref[...], preferred_element_type=F32)

    o_ref[...] = jnp.dot(h_scr[...], w_ref[...],
                         preferred_element_type=F32).astype(BF16)


def _in_proj(x2d, norm_w, w_perm, w_dt, tm=1024, tn=1024):
    m = x2d.shape[0]
    return pl.pallas_call(
        _inproj_body,
        grid=(m // tm, PROJ_WIDTH // tn),
        in_specs=[
            pl.BlockSpec((tm, D_MODEL), lambda i, j: (i, 0)),
            pl.BlockSpec((1, D_MODEL), lambda i, j: (0, 0)),
            pl.BlockSpec((D_MODEL, tn), lambda i, j: (0, j)),
            pl.BlockSpec((D_MODEL, LANES), lambda i, j: (0, 0)),
        ],
        out_specs=[
            pl.BlockSpec((tm, tn), lambda i, j: (i, j)),
            pl.BlockSpec((tm, LANES), lambda i, j: (i, 0)),
        ],
        out_shape=[
            jax.ShapeDtypeStruct((m, PROJ_WIDTH), BF16),
            jax.ShapeDtypeStruct((m, LANES), F32),
        ],
        scratch_shapes=[pltpu.VMEM((tm, D_MODEL), BF16)],
        compiler_params=pltpu.CompilerParams(
            dimension_semantics=("arbitrary", "arbitrary"),
            vmem_limit_bytes=VMEM_LIMIT),
        name="in_proj",
    )(x2d, norm_w, w_perm, w_dt)


def _ssd_body(xbc_ref, z_ref, dt_ref, cw_ref, cb_ref, dtb_ref, aneg_ref,
              dskip_ref, nw_ref, expand_ref, tri_ref, o_ref,
              ext_scr, xs_scr, b_scr, c_scr, state_scr):
    q = SSM_CHUNK
    halo = SUBLANES
    first = pl.program_id(1) == 0

    @pl.when(first)
    def _():
        ext_scr[0:halo, :] = jnp.zeros((halo, SSM_CONV_DIM), F32)
        state_scr[...] = jnp.zeros_like(state_scr)

    @pl.when(jnp.logical_not(first))
    def _():
        ext_scr[0:halo, :] = ext_scr[q:q + halo, :]

    ext_scr[halo:halo + q, :] = xbc_ref[0].astype(F32)

    piece = 512
    for p in range(SSM_CONV_DIM // piece):
        cols = slice(p * piece, (p + 1) * piece)
        acc = jnp.broadcast_to(cb_ref[:, cols], (q, piece))
        for k in range(SSM_CONV):
            lo = halo - (SSM_CONV - 1) + k
            acc = acc + cw_ref[k:k + 1, cols] * ext_scr[lo:lo + q, cols]
        act = _silu(acc)
        if p < 2:
            xs_scr[:, cols] = act
        elif p == 2:
            b_scr[...] = act.astype(BF16)
        else:
            c_scr[...] = act.astype(BF16)

    dt_in = dt_ref[0] + dtb_ref[...]
    dt = jnp.maximum(dt_in, 0.0) + jnp.log(1.0 + jnp.exp(-jnp.abs(dt_in)))
    a = dt * aneg_ref[...]
    cs = jnp.dot(tri_ref[...], a, precision=lax.Precision.HIGHEST,
                 preferred_element_type=F32)
    cs_t = cs.T
    ecs = jnp.exp(cs)
    dec = jnp.exp(cs[q - 1:q, :] - cs)
    stacked = jnp.concatenate(
        [_split_hi_lo(dt), _split_hi_lo(ecs), _split_hi_lo(dec)], axis=0)
    expanded = jnp.dot(stacked, expand_ref[...], preferred_element_type=F32)
    dt_x = expanded[0:q]
    ecs_x = expanded[q:2 * q]
    dec_x = expanded[2 * q:3 * q]

    xs = xs_scr[...]
    xdt = xs * dt_x
    xdt_b = xdt.astype(BF16)
    xdec_b = (xdt * dec_x).astype(BF16)

    row = lax.broadcasted_iota(jnp.int32, (q, q), 0)
    col = lax.broadcasted_iota(jnp.int32, (q, q), 1)
    causal = row >= col

    gw = SSM_HEADS_PER_GROUP * SSM_HEAD_DIM
    for g in range(SSM_GROUPS):
        gcols = slice(g * gw, (g + 1) * gw)
        ncols = slice(g * SSM_STATE, (g + 1) * SSM_STATE)
        bg = b_scr[:, ncols]
        cg = c_scr[:, ncols]
        cb = lax.dot_general(cg, bg, (((1,), (1,)), ((), ())),
                             preferred_element_type=F32)
        st = state_scr[:, gcols]
        y_off = jnp.dot(cg, st.astype(BF16),
                        preferred_element_type=F32) * ecs_x[:, gcols]
        y_parts = []
        for j in range(SSM_HEADS_PER_GROUP):
            h = g * SSM_HEADS_PER_GROUP + j
            seg = cs[:, h:h + 1] - cs_t[h:h + 1, :]
            lmat = jnp.exp(jnp.where(causal, seg, -jnp.inf))
            mh = (cb * lmat).astype(BF16)
            y_parts.append(jnp.dot(
                mh, xdt_b[:, h * SSM_HEAD_DIM:(h + 1) * SSM_HEAD_DIM],
                preferred_element_type=F32))
        y_g = jnp.concatenate(y_parts, axis=1) + y_off
        upd = lax.dot_general(bg, xdec_b[:, gcols], (((0,), (0,)), ((), ())),
                              preferred_element_type=F32)
        state_scr[:, gcols] = st * ecs_x[q - 1:q, gcols] + upd

        y_g = y_g + xs[:, gcols] * dskip_ref[:, gcols]
        yg = y_g * _silu(z_ref[0, :, gcols].astype(F32))
        ms = jnp.mean(yg * yg, axis=-1, keepdims=True)
        yg = yg * lax.rsqrt(ms + NORM_EPS) * nw_ref[:, gcols]
        o_ref[0, :, gcols] = yg.astype(BF16)


def _ssd(proj3, dt3, cw, cb, dtb, aneg, dskip_x, nw, expand_mat, tri):
    bsz, s, _ = proj3.shape
    q = SSM_CHUNK
    const = lambda b, c: (0, 0)
    return pl.pallas_call(
        _ssd_body,
        grid=(bsz, s // q),
        in_specs=[
            pl.BlockSpec((1, q, SSM_CONV_DIM), lambda b, c: (b, c, COL_XBC // SSM_CONV_DIM)),
            pl.BlockSpec((1, q, SSM_INNER), lambda b, c: (b, c, COL_Z // SSM_INNER)),
            pl.BlockSpec((1, q, LANES), lambda b, c: (b, c, 0)),
            pl.BlockSpec((SSM_CONV, SSM_CONV_DIM), const),
            pl.BlockSpec((1, SSM_CONV_DIM), const),
            pl.BlockSpec((1, LANES), const),
            pl.BlockSpec((1, LANES), const),
            pl.BlockSpec((1, SSM_INNER), const),
            pl.BlockSpec((1, SSM_INNER), const),
            pl.BlockSpec((2 * LANES, SSM_INNER), const),
            pl.BlockSpec((q, q), const),
        ],
        out_specs=pl.BlockSpec((1, q, SSM_INNER), lambda b, c: (b, c, 0)),
        out_shape=jax.ShapeDtypeStruct((bsz, s, SSM_INNER), BF16),
        scratch_shapes=[
            pltpu.VMEM((q + 2 * SUBLANES, SSM_CONV_DIM), F32),
            pltpu.VMEM((q, SSM_INNER), F32),
            pltpu.VMEM((q, SSM_GROUPS * SSM_STATE), BF16),
            pltpu.VMEM((q, SSM_GROUPS * SSM_STATE), BF16),
            pltpu.VMEM((SSM_STATE, SSM_INNER), F32),
        ],
        compiler_params=pltpu.CompilerParams(
            dimension_semantics=("arbitrary", "arbitrary"),
            vmem_limit_bytes=VMEM_LIMIT),
        name="ssd",
    )(proj3, proj3, dt3, cw, cb, dtb, aneg, dskip_x, nw, expand_mat, tri)


def _pair_block(q, kp, kc, vp, vc, bias):
    blk = ATTN_BLOCK
    half = ATTN_HEAD_DIM
    low = lax.broadcasted_iota(jnp.int32, (blk, LANES), 1) < half
    q2 = jnp.concatenate([jnp.where(low, q, 0.0), jnp.where(low, 0.0, q)],
                         axis=0).astype(BF16)
    k2 = jnp.concatenate([kp, kc], axis=0).astype(BF16)
    v2 = jnp.concatenate([vp, vc], axis=0).astype(BF16)
    s = lax.dot_general(q2, k2, (((1,), (1,)), ((), ())),
                        preferred_element_type=F32)
    s = s + jnp.concatenate([bias, bias], axis=0)
    mx = jnp.max(s, axis=-1, keepdims=True)
    p = jnp.exp(s - mx).astype(BF16)
    vext = jnp.concatenate([v2, jnp.ones((2 * blk, LANES), BF16)], axis=1)
    r = jnp.dot(p, vext, preferred_element_type=F32)
    acc = jnp.where(low, r[0:blk, 0:LANES], r[blk:, 0:LANES])
    den = jnp.where(low, r[0:blk, LANES:], r[blk:, LANES:])
    mxp = jnp.where(low, mx[0:blk], mx[blk:])
    return acc / den, mxp + jnp.log(den)


def _attn_body(q1_ref, q4_ref, q16_ref, k_ref, v_ref, g_ref, out_ref,
               o4_scr, l4_scr, o16_scr, l16_scr, bias_scr):
    blk = ATTN_BLOCK
    seq = k_ref.shape[1]
    (w1, d1), (w4, d4), (w16, d16) = ATTN_PATTERNS
    n_back = w1 // d1
    assert n_back == w4 // d4 == w16 // d16

    qi = lax.broadcasted_iota(jnp.int32, (blk, 2 * blk), 0) + blk
    kj = lax.broadcasted_iota(jnp.int32, (blk, 2 * blk), 1)
    diff = qi - kj
    band = (diff >= 0) & (diff <= n_back)
    bias_scr[0] = jnp.where(band & (kj >= blk), 0.0, -jnp.inf)
    bias_scr[1] = jnp.where(band, 0.0, -jnp.inf)

    def strided_phase(q_ref, o_scr, l_scr, dil):
        span = blk * dil

        def body(idx, carry):
            r = idx % dil
            n = idx // dil
            cur = r + span * n
            prv = jnp.maximum(cur - span, r)

            def ld(ref, start):
                return ref[0, pl.ds(start, blk, stride=dil), :]

            o, lse = _pair_block(ld(q_ref, cur), ld(k_ref, prv), ld(k_ref, cur),
                                 ld(v_ref, prv), ld(v_ref, cur),
                                 bias_scr[jnp.minimum(n, 1)])
            o_scr[pl.ds(cur, blk, stride=dil), :] = o
            l_scr[pl.ds(cur, blk, stride=dil), :] = lse
            return carry

        lax.fori_loop(0, seq // blk, body, 0, unroll=ATTN_UNROLL)

    strided_phase(q16_ref, o16_scr, l16_scr, d16)
    strided_phase(q4_ref, o4_scr, l4_scr, d4)

    def body(n, carry):
        cur = pl.multiple_of(n * blk, blk)
        prv = pl.multiple_of(jnp.maximum(cur - blk, 0), blk)
        rows = pl.ds(cur, blk)
        prows = pl.ds(prv, blk)
        o1, l1 = _pair_block(q1_ref[0, rows, :], k_ref[0, prows, :], k_ref[0, rows, :],
                             v_ref[0, prows, :], v_ref[0, rows, :],
                             bias_scr[jnp.minimum(n, 1)])
        l4, l16 = l4_scr[rows, :], l16_scr[rows, :]
        lm = jnp.maximum(jnp.maximum(l1, l4), l16)
        e1, e4, e16 = jnp.exp(l1 - lm), jnp.exp(l4 - lm), jnp.exp(l16 - lm)
        y = (e1 * o1 + e4 * o4_scr[rows, :] + e16 * o16_scr[rows, :]) / (e1 + e4 + e16)
        out_ref[0, rows, :] = (y * _silu(g_ref[0, rows, :])).astype(BF16)
        return carry

    lax.fori_loop(0, seq // blk, body, 0, unroll=ATTN_UNROLL)


def _attention(slabs, bsz, seq):
    slab = lambda s0: (lambda b, p: (s0 + p, b, 0))
    spec = lambda s0: pl.BlockSpec((1, seq, LANES), slab(s0))
    return pl.pallas_call(
        _attn_body,
        grid=(bsz, PAIRS),
        in_specs=[spec(SLAB_Q), spec(SLAB_Q + PAIRS), spec(SLAB_Q + 2 * PAIRS),
                  spec(SLAB_K), spec(SLAB_V), spec(SLAB_GATE)],
        out_specs=pl.BlockSpec((1, seq, LANES), lambda b, p: (b, 0, p)),
        out_shape=jax.ShapeDtypeStruct((bsz, seq, ATTN_KV_WIDTH), BF16),
        scratch_shapes=[pltpu.VMEM((seq, LANES), F32) for _ in range(4)]
        + [pltpu.VMEM((2, ATTN_BLOCK, 2 * ATTN_BLOCK), F32)],
        compiler_params=pltpu.CompilerParams(
            dimension_semantics=("arbitrary", "arbitrary"),
            vmem_limit_bytes=VMEM_LIMIT),
        name="dilated_attention",
    )(slabs, slabs, slabs, slabs, slabs, slabs)


def _merge_body(x_ref, sc_ref, m0_ref, m1_ref, m2_ref, yssm_ref, yattn_ref,
                scw_ref, pssm_ref, pattn_ref, psc_ref, wout_ref, nw_ref,
                out_ref, ext_scr, *, tiles_per_seq):
    tm = x_ref.shape[0]
    halo = SUBLANES
    first = (pl.program_id(0) % tiles_per_seq) == 0

    @pl.when(first)
    def _():
        ext_scr[0:halo, :] = jnp.zeros((halo, SC_WIDTH), F32)

    @pl.when(jnp.logical_not(first))
    def _():
        ext_scr[0:halo, :] = ext_scr[tm:tm + halo, :]

    u = sc_ref[:, 0:SC_WIDTH].astype(F32)
    b_sc = sc_ref[:, SC_WIDTH:2 * SC_WIDTH].astype(F32)
    c_sc = sc_ref[:, 2 * SC_WIDTH:3 * SC_WIDTH].astype(F32)
    g_sc = sc_ref[:, 3 * SC_WIDTH:4 * SC_WIDTH].astype(F32)
    ext_scr[halo:halo + tm, :] = c_sc * u
    conv = jnp.zeros((tm, SC_WIDTH), F32)
    for k in range(SC_CONV):
        lo = halo - (SC_CONV - 1) + k
        conv = conv + scw_ref[k:k + 1, :] * ext_scr[lo:lo + tm, :]
    y_sc = (b_sc * conv * _silu(g_sc)).astype(BF16)

    merged = (
        _sigmoid(m0_ref[...].astype(F32))
        * jnp.dot(yssm_ref[...], pssm_ref[...], preferred_element_type=F32)
        + _sigmoid(m1_ref[...].astype(F32))
        * jnp.dot(yattn_ref[...], pattn_ref[...], preferred_element_type=F32)
        + _sigmoid(m2_ref[...].astype(F32))
        * jnp.dot(y_sc, psc_ref[...], preferred_element_type=F32))
    out = jnp.dot(merged.astype(BF16), wout_ref[...], preferred_element_type=F32)
    ms = jnp.mean(out * out, axis=-1, keepdims=True)
    out_ref[...] = x_ref[...] + out * lax.rsqrt(ms + NORM_EPS) * nw_ref[...]


def _merge(x2d, proj, y_ssm, y_attn, scw, p_ssm, p_attn, p_sc, w_out, nw, seq, tm=512):
    m = x2d.shape[0]
    row = lambda c: (lambda i: (i, c))
    const = lambda i: (0, 0)
    wa = ATTN_KV_WIDTH
    return pl.pallas_call(
        functools.partial(_merge_body, tiles_per_seq=seq // tm),
        grid=(m // tm,),
        in_specs=[
            pl.BlockSpec((tm, D_MODEL), row(0)),
            pl.BlockSpec((tm, 4 * SC_WIDTH), row(COL_SC // (4 * SC_WIDTH))),
            pl.BlockSpec((tm, D_MODEL), row(COL_MERGE // D_MODEL)),
            pl.BlockSpec((tm, D_MODEL), row(COL_MERGE // D_MODEL + 1)),
            pl.BlockSpec((tm, D_MODEL), row(COL_MERGE // D_MODEL + 2)),
            pl.BlockSpec((tm, SSM_INNER), row(0)),
            pl.BlockSpec((tm, wa), row(0)),
            pl.BlockSpec((SC_CONV, SC_WIDTH), const),
            pl.BlockSpec((SSM_INNER, D_MODEL), const),
            pl.BlockSpec((wa, D_MODEL), const),
            pl.BlockSpec((SC_WIDTH, D_MODEL), const),
            pl.BlockSpec((D_MODEL, D_MODEL), const),
            pl.BlockSpec((1, D_MODEL), const),
        ],
        out_specs=pl.BlockSpec((tm, D_MODEL), row(0)),
        out_shape=jax.ShapeDtypeStruct((m, D_MODEL), F32),
        scratch_shapes=[pltpu.VMEM((tm + 2 * SUBLANES, SC_WIDTH), F32)],
        compiler_params=pltpu.CompilerParams(
            dimension_semantics=("arbitrary",),
            vmem_limit_bytes=VMEM_LIMIT),
        name="merge_out",
    )(x2d, proj, proj, proj, proj, y_ssm, y_attn,
      scw, p_ssm, p_attn, p_sc, w_out, nw)


def _expand_matrix(n_heads, width):
    m = np.zeros((2 * LANES, n_heads * width), np.float32)
    for h in range(n_heads):
        m[h, h * width:(h + 1) * width] = 1.0
        m[LANES + h, h * width:(h + 1) * width] = 1.0
    return jnp.asarray(m, BF16)


def _pad_lanes(v):
    return jnp.zeros((1, LANES), F32).at[0, :v.shape[0]].set(v.astype(F32))


def _permute_w_in(w):
    offs = np.cumsum([0, SSM_INNER, SSM_CONV_DIM, SSM_HEADS, 3 * ATTN_KV_WIDTH,
                      ATTN_KV_WIDTH, ATTN_KV_WIDTH, ATTN_KV_WIDTH,
                      4 * SC_WIDTH, 3 * D_MODEL])
    z, xbc, dt, q, k, v, gat, sc, mg = [w[:, offs[i]:offs[i + 1]] for i in range(9)]
    scale = ATTN_HEAD_DIM ** -0.5
    w_qkv = jnp.concatenate([q * scale, k, v, gat], axis=1).astype(BF16)
    w_perm = jnp.concatenate([xbc, sc, z, mg], axis=1).astype(BF16)
    w_dt = jnp.zeros((D_MODEL, LANES), F32).at[:, :SSM_HEADS].set(dt).astype(BF16)
    return w_qkv, w_perm, w_dt


def _layer(x2d, bsz, seq, norm_pre, norm_post, w_in, ssm_conv_w, ssm_conv_b, dt_bias,
           a_log, d_skip, ssm_norm, sc_conv_w, p_ssm, p_attn, p_sc, w_out,
           expand_ssm, tri):
    w_qkv, w_perm, w_dt = _permute_w_in(w_in)
    nw_pre = norm_pre.reshape(1, D_MODEL)
    slabs = _qkv_proj(x2d, nw_pre, w_qkv)
    proj, dt_raw = _in_proj(x2d, nw_pre, w_perm, w_dt)
    proj3 = proj.reshape(bsz, seq, PROJ_WIDTH)
    dt3 = dt_raw.reshape(bsz, seq, LANES)

    y_ssm = _ssd(
        proj3, dt3, ssm_conv_w, ssm_conv_b.reshape(1, SSM_CONV_DIM),
        _pad_lanes(dt_bias), _pad_lanes(-jnp.exp(a_log.astype(F32))),
        jnp.repeat(d_skip.astype(F32), SSM_HEAD_DIM).reshape(1, SSM_INNER),
        ssm_norm.reshape(1, SSM_INNER), expand_ssm, tri)

    y_attn = _attention(slabs, bsz, seq)

    return _merge(x2d, proj, y_ssm.reshape(bsz * seq, SSM_INNER),
                  y_attn.reshape(bsz * seq, ATTN_KV_WIDTH),
                  sc_conv_w, p_ssm.astype(BF16), p_attn.astype(BF16),
                  p_sc.astype(BF16), w_out.astype(BF16),
                  norm_post.reshape(1, D_MODEL), seq)


def kernel(x, norm_pre, norm_post, w_in, ssm_conv_w, ssm_conv_b, dt_bias, a_log,
           d_skip, ssm_norm, sc_conv_w, p_ssm, p_attn, p_sc, w_out):
    bsz, seq, _ = x.shape
    expand_ssm = _expand_matrix(SSM_HEADS, SSM_HEAD_DIM)
    tri = jnp.asarray(np.tril(np.ones((SSM_CHUNK, SSM_CHUNK), np.float32)))
    x2d = x.reshape(bsz * seq, D_MODEL)
    for i in range(norm_pre.shape[0]):
        x2d = _layer(x2d, bsz, seq, norm_pre[i], norm_post[i], w_in[i], ssm_conv_w[i],
                     ssm_conv_b[i], dt_bias[i], a_log[i], d_skip[i], ssm_norm[i],
                     sc_conv_w[i], p_ssm[i], p_attn[i], p_sc[i], w_out[i],
                     expand_ssm, tri)
    return x2d.reshape(bsz, seq, D_MODEL)
```

```python
import functools

import jax
import jax.numpy as jnp
import numpy as np
from jax import lax
from jax.experimental import pallas as pl
from jax.experimental.pallas import tpu as pltpu

F32 = jnp.float32
BF16 = jnp.bfloat16

D_MODEL = 1024
SSM_HEADS = 16
SSM_HEAD_DIM = 64
SSM_INNER = SSM_HEADS * SSM_HEAD_DIM
SSM_GROUPS = 4
SSM_HEADS_PER_GROUP = SSM_HEADS // SSM_GROUPS
SSM_STATE = 128
SSM_CONV = 4
SSM_CHUNK = 128
SSM_CONV_DIM = SSM_INNER + 2 * SSM_GROUPS * SSM_STATE
ATTN_HEAD_DIM = 64
ATTN_SLOTS = 8
ATTN_PATTERNS = ((128, 1), (512, 4), (2048, 16))
ATTN_KV_WIDTH = ATTN_SLOTS * ATTN_HEAD_DIM
ATTN_BLOCK = 128
SC_WIDTH = 512
SC_CONV = 3
SSD_SUB = 4
ATTN_UNROLL = 8
NORM_EPS = 1e-6
LOG2E = float(np.log2(np.e))

LANES = 128
SUBLANES = 8
VMEM_LIMIT = 48 * 1024 * 1024

COL_XBC = 0
COL_SC = COL_XBC + SSM_CONV_DIM
COL_Z = COL_SC + 4 * SC_WIDTH
COL_MERGE = COL_Z + SSM_INNER
PROJ_WIDTH = COL_MERGE + 3 * D_MODEL

PAIRS = ATTN_KV_WIDTH // LANES
SLAB_Q = 0
SLAB_K = SLAB_Q + len(ATTN_PATTERNS) * PAIRS
SLAB_V = SLAB_K + PAIRS
SLAB_GATE = SLAB_V + PAIRS
N_SLABS = SLAB_GATE + PAIRS
QKV_WIDTH = N_SLABS * LANES


def _sigmoid(x):
    return 0.5 * jnp.tanh(0.5 * x) + 0.5


def _silu(x):
    hx = 0.5 * x
    return hx * jnp.tanh(hx) + hx


def _split_hi_lo(v):
    hi = v.astype(BF16)
    lo = (v - hi.astype(F32)).astype(BF16)
    return jnp.concatenate([hi, lo], axis=1)


def _normed(x_ref, nw_ref):
    x = x_ref[...]
    ms = jnp.mean(x * x, axis=-1, keepdims=True)
    return (x * lax.rsqrt(ms + NORM_EPS) * nw_ref[...]).astype(BF16)


def _qkvproj_body(x_ref, nw_ref, w_ref, o_ref, h_scr):
    @pl.when(pl.program_id(1) == 0)
    def _():
        h_scr[...] = _normed(x_ref, nw_ref)

    res = jnp.dot(h_scr[...], w_ref[...], preferred_element_type=F32)
    for s in range(o_ref.shape[0]):
        o_ref[s] = res[:, s * LANES:(s + 1) * LANES]


def _qkv_proj(x2d, norm_w, w_qkv, tm=1024, tn=1536):
    m = x2d.shape[0]
    return pl.pallas_call(
        _qkvproj_body,
        grid=(m // tm, QKV_WIDTH // tn),
        in_specs=[
            pl.BlockSpec((tm, D_MODEL), lambda i, j: (i, 0)),
            pl.BlockSpec((1, D_MODEL), lambda i, j: (0, 0)),
            pl.BlockSpec((D_MODEL, tn), lambda i, j: (0, j)),
        ],
        out_specs=pl.BlockSpec((tn // LANES, tm, LANES), lambda i, j: (j, i, 0)),
        out_shape=jax.ShapeDtypeStruct((N_SLABS, m, LANES), F32),
        scratch_shapes=[pltpu.VMEM((tm, D_MODEL), BF16)],
        compiler_params=pltpu.CompilerParams(
            dimension_semantics=("arbitrary", "arbitrary"),
            vmem_limit_bytes=VMEM_LIMIT),
        name="qkv_proj",
    )(x2d, norm_w, w_qkv)


def _inproj_body(x_ref, nw_ref, w_ref, wdt_ref, o_ref, dt_ref, h_scr):
    @pl.when(pl.program_id(1) == 0)
    def _():
        h = _normed(x_ref, nw_ref)
        h_scr[...] = h
        dt_ref[...] = jnp.dot(h, wdt_ref[...], preferred_element_type=F32)

    o_ref[...] = jnp.dot(h_scr[...], w_ref[...],
                         preferred_element_type=F32).astype(BF16)


def _in_proj(x2d, norm_w, w_perm, w_dt, tm=1024, tn=2048):
    m = x2d.shape[0]
    return pl.pallas_call(
        _inproj_body,
        grid=(m // tm, PROJ_WIDTH // tn),
        in_specs=[
            pl.BlockSpec((tm, D_MODEL), lambda i, j: (i, 0)),
            pl.BlockSpec((1, D_MODEL), lambda i, j: (0, 0)),
            pl.BlockSpec((D_MODEL, tn), lambda i, j: (0, j)),
            pl.BlockSpec((D_MODEL, LANES), lambda i, j: (0, 0)),
        ],
        out_specs=[
            pl.BlockSpec((tm, tn), lambda i, j: (i, j)),
            pl.BlockSpec((tm, LANES), lambda i, j: (i, 0)),
        ],
        out_shape=[
            jax.ShapeDtypeStruct((m, PROJ_WIDTH), BF16),
            jax.ShapeDtypeStruct((m, LANES), F32),
        ],
        scratch_shapes=[pltpu.VMEM((tm, D_MODEL), BF16)],
        compiler_params=pltpu.CompilerParams(
            dimension_semantics=("arbitrary", "arbitrary"),
            vmem_limit_bytes=VMEM_LIMIT),
        name="in_proj",
    )(x2d, norm_w, w_perm, w_dt)


def _ssd_body(xbc_ref, z_ref, dt_ref, cw_ref, cb_ref, dtb_ref, aneg_ref,
              dskip_ref, nw_ref, expand_ref, tri_ref, shift_ref, o_ref,
              prev_scr, xs_scr, b_scr, c_scr, state_scr):
    q = SSM_CHUNK

    @pl.when(pl.program_id(1) == 0)
    def _():
        prev_scr[...] = jnp.zeros_like(prev_scr)
        state_scr[...] = jnp.zeros_like(state_scr)

    row = lax.broadcasted_iota(jnp.int32, (q, q), 0)
    col = lax.broadcasted_iota(jnp.int32, (q, q), 1)
    causal = row >= col

    for sub in range(SSD_SUB):
        rows = slice(sub * q, (sub + 1) * q)
        _ssd_chunk(
            xbc_ref.at[0, rows, :],
            prev_scr if sub == 0 else xbc_ref.at[0, slice((sub - 1) * q, sub * q), :],
            z_ref.at[0, rows, :], dt_ref.at[0, rows, :], cw_ref, cb_ref, dtb_ref, aneg_ref,
            dskip_ref, nw_ref, expand_ref, tri_ref, shift_ref, o_ref.at[0, rows, :],
            xs_scr.at[rows, :], b_scr.at[rows, :], c_scr.at[rows, :], state_scr, causal)
    prev_scr[...] = xbc_ref[0, (SSD_SUB - 1) * q:SSD_SUB * q, :]


def _ssd_chunk(xbc_ref, prev_ref, z_ref, dt_ref, cw_ref, cb_ref, dtb_ref, aneg_ref,
               dskip_ref, nw_ref, expand_ref, tri_ref, shift_ref, o_ref,
               xs_scr, b_scr, c_scr, state_scr, causal):
    q = SSM_CHUNK

    piece = 512
    for p in range(SSM_CONV_DIM // piece):
        cols = slice(p * piece, (p + 1) * piece)
        cur = xbc_ref[:, cols]
        both = jnp.concatenate([prev_ref[:, cols], cur], axis=0)
        shifted = jnp.dot(shift_ref[...], both, preferred_element_type=F32)
        acc = cb_ref[:, cols] + cw_ref[SSM_CONV - 1:SSM_CONV, cols] * cur.astype(F32)
        for k in range(SSM_CONV - 1):
            acc = acc + cw_ref[k:k + 1, cols] * shifted[k * q:(k + 1) * q]
        act = _silu(acc)
        if p < 2:
            xs_scr[:, cols] = act
        elif p == 2:
            b_scr[...] = act.astype(BF16)
        else:
            c_scr[...] = act.astype(BF16)

    dt_in = dt_ref[...] + dtb_ref[...]
    dt = jnp.maximum(dt_in, 0.0) + jnp.log(1.0 + jnp.exp(-jnp.abs(dt_in)))
    a = dt * aneg_ref[...]
    cs = jnp.dot(tri_ref[...], a, precision=lax.Precision.HIGHEST,
                 preferred_element_type=F32)
    cs2 = cs * LOG2E
    cs2_t = cs2.T
    ecs = jnp.exp(cs)
    dec = jnp.exp(cs[q - 1:q, :] - cs)
    stacked = jnp.concatenate(
        [_split_hi_lo(dt), _split_hi_lo(ecs), _split_hi_lo(dec)], axis=0)
    expanded = jnp.dot(stacked, expand_ref[...], preferred_element_type=F32)
    dt_x = expanded[0:q]
    ecs_x = expanded[q:2 * q]
    dec_x = expanded[2 * q:3 * q]

    xs = xs_scr[...]
    xdt = xs * dt_x
    xdt_b = xdt.astype(BF16)
    xdec_b = (xdt * dec_x).astype(BF16)

    gw = SSM_HEADS_PER_GROUP * SSM_HEAD_DIM
    for g in range(SSM_GROUPS):
        gcols = slice(g * gw, (g + 1) * gw)
        ncols = slice(g * SSM_STATE, (g + 1) * SSM_STATE)
        bg = b_scr[:, ncols]
        cg = c_scr[:, ncols]
        cb = lax.dot_general(cg, bg, (((1,), (1,)), ((), ())),
                             preferred_element_type=F32)
        st = state_scr[:, gcols]
        y_off = jnp.dot(cg, st.astype(BF16),
                        preferred_element_type=F32) * ecs_x[:, gcols]
        y_parts = []
        for j in range(SSM_HEADS_PER_GROUP):
            h = g * SSM_HEADS_PER_GROUP + j
            seg = cs2[:, h:h + 1] - cs2_t[h:h + 1, :]
            lmat = jnp.exp2(jnp.where(causal, seg, -jnp.inf))
            mh = (cb * lmat).astype(BF16)
            y_parts.append(jnp.dot(
                mh, xdt_b[:, h * SSM_HEAD_DIM:(h + 1) * SSM_HEAD_DIM],
                preferred_element_type=F32))
        y_g = jnp.concatenate(y_parts, axis=1) + y_off
        upd = lax.dot_general(bg, xdec_b[:, gcols], (((0,), (0,)), ((), ())),
                              preferred_element_type=F32)
        state_scr[:, gcols] = st * ecs_x[q - 1:q, gcols] + upd

        y_g = y_g + xs[:, gcols] * dskip_ref[:, gcols]
        yg = y_g * _silu(z_ref[:, gcols].astype(F32))
        ms = jnp.mean(yg * yg, axis=-1, keepdims=True)
        yg = yg * lax.rsqrt(ms + NORM_EPS) * nw_ref[:, gcols]
        o_ref[:, gcols] = yg.astype(BF16)


def _shift_matrix():
    q = SSM_CHUNK
    m = np.zeros(((SSM_CONV - 1) * q, 2 * q), np.float32)
    for k in range(SSM_CONV - 1):
        for t in range(q):
            m[k * q + t, q + t - (SSM_CONV - 1) + k] = 1.0
    return jnp.asarray(m, BF16)


def _ssd(proj3, dt3, cw, cb, dtb, aneg, dskip_x, nw, expand_mat, tri, shift):
    bsz, s, _ = proj3.shape
    q = SSM_CHUNK
    rows = SSD_SUB * q
    const = lambda b, c: (0, 0)
    return pl.pallas_call(
        _ssd_body,
        grid=(bsz, s // rows),
        in_specs=[
            pl.BlockSpec((1, rows, SSM_CONV_DIM), lambda b, c: (b, c, COL_XBC // SSM_CONV_DIM)),
            pl.BlockSpec((1, rows, SSM_INNER), lambda b, c: (b, c, COL_Z // SSM_INNER)),
            pl.BlockSpec((1, rows, LANES), lambda b, c: (b, c, 0)),
            pl.BlockSpec((SSM_CONV, SSM_CONV_DIM), const),
            pl.BlockSpec((1, SSM_CONV_DIM), const),
            pl.BlockSpec((1, LANES), const),
            pl.BlockSpec((1, LANES), const),
            pl.BlockSpec((1, SSM_INNER), const),
            pl.BlockSpec((1, SSM_INNER), const),
            pl.BlockSpec((2 * LANES, SSM_INNER), const),
            pl.BlockSpec((q, q), const),
            pl.BlockSpec(((SSM_CONV - 1) * q, 2 * q), const),
        ],
        out_specs=pl.BlockSpec((1, rows, SSM_INNER), lambda b, c: (b, c, 0)),
        out_shape=jax.ShapeDtypeStruct((bsz, s, SSM_INNER), BF16),
        scratch_shapes=[
            pltpu.VMEM((q, SSM_CONV_DIM), BF16),
            pltpu.VMEM((rows, SSM_INNER), F32),
            pltpu.VMEM((rows, SSM_GROUPS * SSM_STATE), BF16),
            pltpu.VMEM((rows, SSM_GROUPS * SSM_STATE), BF16),
            pltpu.VMEM((SSM_STATE, SSM_INNER), F32),
        ],
        compiler_params=pltpu.CompilerParams(
            dimension_semantics=("arbitrary", "arbitrary"),
            vmem_limit_bytes=VMEM_LIMIT),
        name="ssd",
    )(proj3, proj3, dt3, cw, cb, dtb, aneg, dskip_x, nw, expand_mat, tri, shift)


def _attn_block(q, k, v, bias):
    nq, nk = bias.shape
    low = lax.broadcasted_iota(jnp.int32, (nq, LANES), 1) < ATTN_HEAD_DIM
    q2 = jnp.concatenate([jnp.where(low, q, 0.0), jnp.where(low, 0.0, q)],
                         axis=0).astype(BF16)
    s = lax.dot_general(q2, k.astype(BF16), (((1,), (1,)), ((), ())),
                        preferred_element_type=F32)
    s = s + jnp.concatenate([bias, bias], axis=0)
    mx = jnp.max(s, axis=-1, keepdims=True)
    p = jnp.exp2(s - mx).astype(BF16)
    vext = jnp.concatenate([v.astype(BF16), jnp.ones((nk, LANES), BF16)], axis=1)
    r = jnp.dot(p, vext, preferred_element_type=F32)
    acc = jnp.where(low, r[0:nq, 0:LANES], r[nq:, 0:LANES])
    den = jnp.where(low, r[0:nq, LANES:], r[nq:, LANES:])
    mxp = jnp.where(low, mx[0:nq], mx[nq:])
    return acc / den, mxp + jnp.log2(den)


def _attn_body(q1_ref, q4_ref, q16_ref, k_ref, v_ref, g_ref, out_ref,
               o4_scr, l4_scr, o16_scr, l16_scr, bias_scr, wbias_scr):
    blk = ATTN_BLOCK
    seq = k_ref.shape[1]
    (w1, d1), (w4, d4), (w16, d16) = ATTN_PATTERNS
    n_back = w1 // d1
    assert n_back == w4 // d4 == w16 // d16 == blk

    qi = lax.broadcasted_iota(jnp.int32, (blk, 2 * blk), 0) + blk
    kj = lax.broadcasted_iota(jnp.int32, (blk, 2 * blk), 1)
    diff = qi - kj
    band = (diff >= 0) & (diff <= n_back)
    bias_scr[0] = jnp.where(band & (kj >= blk), 0.0, -jnp.inf)
    bias_scr[1] = jnp.where(band, 0.0, -jnp.inf)
    wdiff = (lax.broadcasted_iota(jnp.int32, (2 * blk, 2 * blk), 0)
             - lax.broadcasted_iota(jnp.int32, (2 * blk, 2 * blk), 1))
    wbias_scr[...] = jnp.where((wdiff >= 0) & (wdiff <= n_back), 0.0, -jnp.inf)

    def whole_phase(q_ref, o_scr, l_scr, dil):
        length = seq // dil
        assert length == 2 * blk

        def body(r, carry):
            rows = pl.ds(r, length, stride=dil)
            o, lse = _attn_block(q_ref[0, rows, :], k_ref[0, rows, :], v_ref[0, rows, :],
                                 wbias_scr[...])
            o_scr[rows, :] = o
            l_scr[rows, :] = lse
            return carry

        lax.fori_loop(0, dil, body, 0, unroll=ATTN_UNROLL // 2)

    def strided_phase(q_ref, o_scr, l_scr, dil):
        span = blk * dil

        def body(idx, carry):
            r = idx % dil
            n = idx // dil
            cur = r + span * n
            prv = jnp.maximum(cur - span, r)

            def ld(ref, start):
                return ref[0, pl.ds(start, blk, stride=dil), :]

            o, lse = _attn_block(
                ld(q_ref, cur),
                jnp.concatenate([ld(k_ref, prv), ld(k_ref, cur)], axis=0),
                jnp.concatenate([ld(v_ref, prv), ld(v_ref, cur)], axis=0),
                bias_scr[jnp.minimum(n, 1)])
            o_scr[pl.ds(cur, blk, stride=dil), :] = o
            l_scr[pl.ds(cur, blk, stride=dil), :] = lse
            return carry

        lax.fori_loop(0, seq // blk, body, 0, unroll=ATTN_UNROLL)

    whole_phase(q16_ref, o16_scr, l16_scr, d16)
    strided_phase(q4_ref, o4_scr, l4_scr, d4)

    def body(n, carry):
        cur = pl.multiple_of(n * blk, blk)
        prv = pl.multiple_of(jnp.maximum(cur - blk, 0), blk)
        rows = pl.ds(cur, blk)
        prows = pl.ds(prv, blk)
        o1, l1 = _attn_block(
            q1_ref[0, rows, :],
            jnp.concatenate([k_ref[0, prows, :], k_ref[0, rows, :]], axis=0),
            jnp.concatenate([v_ref[0, prows, :], v_ref[0, rows, :]], axis=0),
            bias_scr[jnp.minimum(n, 1)])
        l4, l16 = l4_scr[rows, :], l16_scr[rows, :]
        lm = jnp.maximum(jnp.maximum(l1, l4), l16)
        e1, e4, e16 = jnp.exp2(l1 - lm), jnp.exp2(l4 - lm), jnp.exp2(l16 - lm)
        y = (e1 * o1 + e4 * o4_scr[rows, :] + e16 * o16_scr[rows, :]) / (e1 + e4 + e16)
        out_ref[0, rows, :] = (y * _silu(g_ref[0, rows, :])).astype(BF16)
        return carry

    lax.fori_loop(0, seq // blk, body, 0, unroll=ATTN_UNROLL)


def _attention(slabs, bsz, seq):
    slab = lambda s0: (lambda b, p: (s0 + p, b, 0))
    spec = lambda s0: pl.BlockSpec((1, seq, LANES), slab(s0))
    return pl.pallas_call(
        _attn_body,
        grid=(bsz, PAIRS),
        in_specs=[spec(SLAB_Q), spec(SLAB_Q + PAIRS), spec(SLAB_Q + 2 * PAIRS),
                  spec(SLAB_K), spec(SLAB_V), spec(SLAB_GATE)],
        out_specs=pl.BlockSpec((1, seq, LANES), lambda b, p: (b, 0, p)),
        out_shape=jax.ShapeDtypeStruct((bsz, seq, ATTN_KV_WIDTH), BF16),
        scratch_shapes=[pltpu.VMEM((seq, LANES), F32) for _ in range(4)]
        + [pltpu.VMEM((2, ATTN_BLOCK, 2 * ATTN_BLOCK), F32),
           pltpu.VMEM((2 * ATTN_BLOCK, 2 * ATTN_BLOCK), F32)],
        compiler_params=pltpu.CompilerParams(
            dimension_semantics=("arbitrary", "arbitrary"),
            vmem_limit_bytes=VMEM_LIMIT),
        name="dilated_attention",
    )(slabs, slabs, slabs, slabs, slabs, slabs)


def _merge_body(x_ref, sc_ref, m0_ref, m1_ref, m2_ref, yssm_ref, yattn_ref,
                scw_ref, pssm_ref, pattn_ref, psc_ref, wout_ref, nw_ref,
                out_ref, ext_scr, *, tiles_per_seq):
    tm = x_ref.shape[0]
    halo = SUBLANES
    first = (pl.program_id(0) % tiles_per_seq) == 0

    @pl.when(first)
    def _():
        ext_scr[0:halo, :] = jnp.zeros((halo, SC_WIDTH), F32)

    @pl.when(jnp.logical_not(first))
    def _():
        ext_scr[0:halo, :] = ext_scr[tm:tm + halo, :]

    u = sc_ref[:, 0:SC_WIDTH].astype(F32)
    b_sc = sc_ref[:, SC_WIDTH:2 * SC_WIDTH].astype(F32)
    c_sc = sc_ref[:, 2 * SC_WIDTH:3 * SC_WIDTH].astype(F32)
    g_sc = sc_ref[:, 3 * SC_WIDTH:4 * SC_WIDTH].astype(F32)
    ext_scr[halo:halo + tm, :] = c_sc * u
    conv = jnp.zeros((tm, SC_WIDTH), F32)
    for k in range(SC_CONV):
        lo = halo - (SC_CONV - 1) + k
        conv = conv + scw_ref[k:k + 1, :] * ext_scr[lo:lo + tm, :]
    y_sc = (b_sc * conv * _silu(g_sc)).astype(BF16)

    merged = (
        _sigmoid(m0_ref[...].astype(F32))
        * jnp.dot(yssm_ref[...], pssm_ref[...], preferred_element_type=F32)
        + _sigmoid(m1_ref[...].astype(F32))
        * jnp.dot(yattn_ref[...], pattn_ref[...], preferred_element_type=F32)
        + _sigmoid(m2_ref[...].astype(F32))
        * jnp.dot(y_sc, psc_ref[...], preferred_element_type=F32))
    out = jnp.dot(merged.astype(BF16), wout_ref[...], preferred_element_type=F32)
    ms = jnp.mean(out * out, axis=-1, keepdims=True)
    out_ref[...] = x_ref[...] + out * lax.rsqrt(ms + NORM_EPS) * nw_ref[...]


def _merge(x2d, proj, y_ssm, y_attn, scw, p_ssm, p_attn, p_sc, w_out, nw, seq, tm=512):
    m = x2d.shape[0]
    row = lambda c: (lambda i: (i, c))
    const = lambda i: (0, 0)
    wa = ATTN_KV_WIDTH
    return pl.pallas_call(
        functools.partial(_merge_body, tiles_per_seq=seq // tm),
        grid=(m // tm,),
        in_specs=[
            pl.BlockSpec((tm, D_MODEL), row(0)),
            pl.BlockSpec((tm, 4 * SC_WIDTH), row(COL_SC // (4 * SC_WIDTH))),
            pl.BlockSpec((tm, D_MODEL), row(COL_MERGE // D_MODEL)),
            pl.BlockSpec((tm, D_MODEL), row(COL_MERGE // D_MODEL + 1)),
            pl.BlockSpec((tm, D_MODEL), row(COL_MERGE // D_MODEL + 2)),
            pl.BlockSpec((tm, SSM_INNER), row(0)),
            pl.BlockSpec((tm, wa), row(0)),
            pl.BlockSpec((SC_CONV, SC_WIDTH), const),
            pl.BlockSpec((SSM_INNER, D_MODEL), const),
            pl.BlockSpec((wa, D_MODEL), const),
            pl.BlockSpec((SC_WIDTH, D_MODEL), const),
            pl.BlockSpec((D_MODEL, D_MODEL), const),
            pl.BlockSpec((1, D_MODEL), const),
        ],
        out_specs=pl.BlockSpec((tm, D_MODEL), row(0)),
        out_shape=jax.ShapeDtypeStruct((m, D_MODEL), F32),
        scratch_shapes=[pltpu.VMEM((tm + 2 * SUBLANES, SC_WIDTH), F32)],
        compiler_params=pltpu.CompilerParams(
            dimension_semantics=("arbitrary",),
            vmem_limit_bytes=VMEM_LIMIT),
        name="merge_out",
    )(x2d, proj, proj, proj, proj, y_ssm, y_attn,
      scw, p_ssm, p_attn, p_sc, w_out, nw)


def _expand_matrix(n_heads, width):
    m = np.zeros((2 * LANES, n_heads * width), np.float32)
    for h in range(n_heads):
        m[h, h * width:(h + 1) * width] = 1.0
        m[LANES + h, h * width:(h + 1) * width] = 1.0
    return jnp.asarray(m, BF16)


def _pad_lanes(v):
    return jnp.zeros((1, LANES), F32).at[0, :v.shape[0]].set(v.astype(F32))


def _permute_w_in(w):
    offs = np.cumsum([0, SSM_INNER, SSM_CONV_DIM, SSM_HEADS, 3 * ATTN_KV_WIDTH,
                      ATTN_KV_WIDTH, ATTN_KV_WIDTH, ATTN_KV_WIDTH,
                      4 * SC_WIDTH, 3 * D_MODEL])
    z, xbc, dt, q, k, v, gat, sc, mg = [w[:, offs[i]:offs[i + 1]] for i in range(9)]
    scale = ATTN_HEAD_DIM ** -0.5 * LOG2E
    w_qkv = jnp.concatenate([q * scale, k, v, gat], axis=1).astype(BF16)
    w_perm = jnp.concatenate([xbc, sc, z, mg], axis=1).astype(BF16)
    w_dt = jnp.zeros((D_MODEL, LANES), F32).at[:, :SSM_HEADS].set(dt).astype(BF16)
    return w_qkv, w_perm, w_dt


def _layer(x2d, bsz, seq, norm_pre, norm_post, w_in, ssm_conv_w, ssm_conv_b, dt_bias,
           a_log, d_skip, ssm_norm, sc_conv_w, p_ssm, p_attn, p_sc, w_out,
           expand_ssm, tri, shift):
    w_qkv, w_perm, w_dt = _permute_w_in(w_in)
    nw_pre = norm_pre.reshape(1, D_MODEL)
    slabs = _qkv_proj(x2d, nw_pre, w_qkv)
    proj, dt_raw = _in_proj(x2d, nw_pre, w_perm, w_dt)
    proj3 = proj.reshape(bsz, seq, PROJ_WIDTH)
    dt3 = dt_raw.reshape(bsz, seq, LANES)

    y_ssm = _ssd(
        proj3, dt3, ssm_conv_w, ssm_conv_b.reshape(1, SSM_CONV_DIM),
        _pad_lanes(dt_bias), _pad_lanes(-jnp.exp(a_log.astype(F32))),
        jnp.repeat(d_skip.astype(F32), SSM_HEAD_DIM).reshape(1, SSM_INNER),
        ssm_norm.reshape(1, SSM_INNER), expand_ssm, tri, shift)

    y_attn = _attention(slabs, bsz, seq)

    return _merge(x2d, proj, y_ssm.reshape(bsz * seq, SSM_INNER),
                  y_attn.reshape(bsz * seq, ATTN_KV_WIDTH),
                  sc_conv_w, p_ssm.astype(BF16), p_attn.astype(BF16),
                  p_sc.astype(BF16), w_out.astype(BF16),
                  norm_post.reshape(1, D_MODEL), seq)


def kernel(x, norm_pre, norm_post, w_in, ssm_conv_w, ssm_conv_b, dt_bias, a_log,
           d_skip, ssm_norm, sc_conv_w, p_ssm, p_attn, p_sc, w_out):
    bsz, seq, _ = x.shape
    expand_ssm = _expand_matrix(SSM_HEADS, SSM_HEAD_DIM)
    tri = jnp.asarray(np.tril(np.ones((SSM_CHUNK, SSM_CHUNK), np.float32)))
    shift = _shift_matrix()
    x2d = x.reshape(bsz * seq, D_MODEL)
    for i in range(norm_pre.shape[0]):
        x2d = _layer(x2d, bsz, seq, norm_pre[i], norm_post[i], w_in[i], ssm_conv_w[i],
                     ssm_conv_b[i], dt_bias[i], a_log[i], d_skip[i], ssm_norm[i],
                     sc_conv_w[i], p_ssm[i], p_attn[i], p_sc[i], w_out[i],
                     expand_ssm, tri, shift)
    return x2d.reshape(bsz, seq, D_MODEL)
```

```python
import functools

import jax
import jax.numpy as jnp
import numpy as np
from jax import lax
from jax.experimental import pallas as pl
from jax.experimental.pallas import tpu as pltpu

F32 = jnp.float32
BF16 = jnp.bfloat16

D_MODEL = 1024
SSM_HEADS = 16
SSM_HEAD_DIM = 64
SSM_INNER = SSM_HEADS * SSM_HEAD_DIM
SSM_GROUPS = 4
SSM_HEADS_PER_GROUP = SSM_HEADS // SSM_GROUPS
SSM_STATE = 128
SSM_CONV = 4
SSM_CHUNK = 128
SSM_CONV_DIM = SSM_INNER + 2 * SSM_GROUPS * SSM_STATE
ATTN_HEAD_DIM = 64
ATTN_SLOTS = 8
ATTN_PATTERNS = ((128, 1), (512, 4), (2048, 16))
ATTN_KV_WIDTH = ATTN_SLOTS * ATTN_HEAD_DIM
ATTN_BLOCK = 128
SC_WIDTH = 512
SC_CONV = 3
SSD_SUB = 4
ATTN_UNROLL = 8
NORM_EPS = 1e-6
LOG2E = float(np.log2(np.e))

LANES = 128
SUBLANES = 8
VMEM_LIMIT = 48 * 1024 * 1024

COL_XBC = 0
COL_SC = COL_XBC + SSM_CONV_DIM
COL_Z = COL_SC + 4 * SC_WIDTH
COL_MERGE = COL_Z + SSM_INNER
PROJ_WIDTH = COL_MERGE + 3 * D_MODEL

PAIRS = ATTN_KV_WIDTH // LANES
SLAB_Q1_Q4 = 0
SLAB_Q16_GATE = SLAB_Q1_Q4 + PAIRS
SLAB_K_V = SLAB_Q16_GATE + PAIRS
N_SLABS = SLAB_K_V + PAIRS
QKV_WIDTH = 2 * N_SLABS * LANES
HIGH_HALF = -65536


def _half_silu(hx):
    return hx * jnp.tanh(hx) + hx


def _half_gate2(hx):
    return jnp.tanh(hx) + 1.0


def _pack_pair(a, b):
    ab = lax.bitcast_convert_type(a.astype(BF16).astype(F32), jnp.int32)
    bb = lax.bitcast_convert_type(b.astype(BF16).astype(F32), jnp.int32)
    return ab | lax.shift_right_logical(bb, 16)


def _unpack_high(w):
    return lax.bitcast_convert_type(w & HIGH_HALF, F32)


def _unpack_low(w):
    return lax.bitcast_convert_type(lax.shift_left(w, 16), F32)


def _split_hi_lo(v):
    hi = v.astype(BF16)
    lo = (v - hi.astype(F32)).astype(BF16)
    return jnp.concatenate([hi, lo], axis=1)


def _normed(x_ref, nw_ref):
    x = x_ref[...]
    ms = jnp.mean(x * x, axis=-1, keepdims=True)
    return (x * lax.rsqrt(ms + NORM_EPS) * nw_ref[...]).astype(BF16)


def _qkvproj_body(x_ref, nw_ref, w_ref, o_ref):
    res = jnp.dot(_normed(x_ref, nw_ref), w_ref[...], preferred_element_type=F32)
    for s in range(N_SLABS):
        hi = res[:, (2 * s) * LANES:(2 * s + 1) * LANES]
        lo = res[:, (2 * s + 1) * LANES:(2 * s + 2) * LANES]
        o_ref[s] = _pack_pair(hi, lo)


def _qkv_proj(x2d, norm_w, w_qkv, tm=512):
    m = x2d.shape[0]
    return pl.pallas_call(
        _qkvproj_body,
        grid=(m // tm,),
        in_specs=[
            pl.BlockSpec((tm, D_MODEL), lambda i: (i, 0)),
            pl.BlockSpec((1, D_MODEL), lambda i: (0, 0)),
            pl.BlockSpec((D_MODEL, QKV_WIDTH), lambda i: (0, 0)),
        ],
        out_specs=pl.BlockSpec((N_SLABS, tm, LANES), lambda i: (0, i, 0)),
        out_shape=jax.ShapeDtypeStruct((N_SLABS, m, LANES), jnp.int32),
        compiler_params=pltpu.CompilerParams(
            dimension_semantics=("arbitrary",),
            vmem_limit_bytes=VMEM_LIMIT),
        name="qkv_proj",
    )(x2d, norm_w, w_qkv)


def _inproj_body(x_ref, nw_ref, w_ref, wdt_ref, o_ref, dt_ref, h_scr):
    @pl.when(pl.program_id(1) == 0)
    def _():
        h = _normed(x_ref, nw_ref)
        h_scr[...] = h
        dt_ref[...] = jnp.dot(h, wdt_ref[...], preferred_element_type=F32)

    o_ref[...] = jnp.dot(h_scr[...], w_ref[...],
                         preferred_element_type=F32).astype(BF16)


def _in_proj(x2d, norm_w, w_perm, w_dt, tm=1024, tn=2048):
    m = x2d.shape[0]
    return pl.pallas_call(
        _inproj_body,
        grid=(m // tm, PROJ_WIDTH // tn),
        in_specs=[
            pl.BlockSpec((tm, D_MODEL), lambda i, j: (i, 0)),
            pl.BlockSpec((1, D_MODEL), lambda i, j: (0, 0)),
            pl.BlockSpec((D_MODEL, tn), lambda i, j: (0, j)),
            pl.BlockSpec((D_MODEL, LANES), lambda i, j: (0, 0)),
        ],
        out_specs=[
            pl.BlockSpec((tm, tn), lambda i, j: (i, j)),
            pl.BlockSpec((tm, LANES), lambda i, j: (i, 0)),
        ],
        out_shape=[
            jax.ShapeDtypeStruct((m, PROJ_WIDTH), BF16),
            jax.ShapeDtypeStruct((m, LANES), F32),
        ],
        scratch_shapes=[pltpu.VMEM((tm, D_MODEL), BF16)],
        compiler_params=pltpu.CompilerParams(
            dimension_semantics=("arbitrary", "arbitrary"),
            vmem_limit_bytes=VMEM_LIMIT),
        name="in_proj",
    )(x2d, norm_w, w_perm, w_dt)


def _ssd_body(xbc_ref, z_ref, dt_ref, cw_ref, cb_ref, dtb_ref, aneg_ref,
              dskip_ref, nw_ref, expand_ref, tri_ref, shift_ref, o_ref,
              prev_scr, xs_scr, b_scr, c_scr, state_scr):
    q = SSM_CHUNK

    @pl.when(pl.program_id(1) == 0)
    def _():
        prev_scr[...] = jnp.zeros_like(prev_scr)
        state_scr[...] = jnp.zeros_like(state_scr)

    row = lax.broadcasted_iota(jnp.int32, (q, q), 0)
    col = lax.broadcasted_iota(jnp.int32, (q, q), 1)
    causal = row >= col

    for sub in range(SSD_SUB):
        rows = slice(sub * q, (sub + 1) * q)
        _ssd_chunk(
            xbc_ref.at[0, rows, :],
            prev_scr if sub == 0 else xbc_ref.at[0, slice((sub - 1) * q, sub * q), :],
            z_ref.at[0, rows, :], dt_ref.at[0, rows, :], cw_ref, cb_ref, dtb_ref, aneg_ref,
            dskip_ref, nw_ref, expand_ref, tri_ref, shift_ref, o_ref.at[0, rows, :],
            xs_scr.at[rows, :], b_scr.at[rows, :], c_scr.at[rows, :], state_scr, causal)
    prev_scr[...] = xbc_ref[0, (SSD_SUB - 1) * q:SSD_SUB * q, :]


def _ssd_chunk(xbc_ref, prev_ref, z_ref, dt_ref, cw_ref, cb_ref, dtb_ref, aneg_ref,
               dskip_ref, nw_ref, expand_ref, tri_ref, shift_ref, o_ref,
               xs_scr, b_scr, c_scr, state_scr, causal):
    q = SSM_CHUNK

    piece = 512
    for p in range(SSM_CONV_DIM // piece):
        cols = slice(p * piece, (p + 1) * piece)
        cur = xbc_ref[:, cols]
        both = jnp.concatenate([prev_ref[:, cols], cur], axis=0)
        shifted = jnp.dot(shift_ref[...], both, preferred_element_type=F32)
        acc = cb_ref[:, cols] + cw_ref[SSM_CONV - 1:SSM_CONV, cols] * cur.astype(F32)
        for k in range(SSM_CONV - 1):
            acc = acc + cw_ref[k:k + 1, cols] * shifted[k * q:(k + 1) * q]
        act = _half_silu(acc)
        if p < 2:
            xs_scr[:, cols] = act
        elif p == 2:
            b_scr[...] = act.astype(BF16)
        else:
            c_scr[...] = act.astype(BF16)

    dt_in = dt_ref[...] + dtb_ref[...]
    dt = jnp.maximum(dt_in, 0.0) + jnp.log(1.0 + jnp.exp(-jnp.abs(dt_in)))
    a = dt * aneg_ref[...]
    cs = jnp.dot(tri_ref[...], a, precision=lax.Precision.HIGHEST,
                 preferred_element_type=F32)
    cs2 = cs * LOG2E
    cs2_t = cs2.T
    ecs = jnp.exp(cs)
    dec = jnp.exp(cs[q - 1:q, :] - cs)
    stacked = jnp.concatenate(
        [_split_hi_lo(dt), _split_hi_lo(ecs), _split_hi_lo(dec)], axis=0)
    expanded = jnp.dot(stacked, expand_ref[...], preferred_element_type=F32)
    dt_x = expanded[0:q]
    ecs_x = expanded[q:2 * q]
    dec_x = expanded[2 * q:3 * q]

    xs = xs_scr[...]
    xdt = xs * dt_x
    xdt_b = xdt.astype(BF16)
    xdec_b = (xdt * dec_x).astype(BF16)

    gw = SSM_HEADS_PER_GROUP * SSM_HEAD_DIM
    for g in range(SSM_GROUPS):
        gcols = slice(g * gw, (g + 1) * gw)
        ncols = slice(g * SSM_STATE, (g + 1) * SSM_STATE)
        bg = b_scr[:, ncols]
        cg = c_scr[:, ncols]
        cb = lax.dot_general(cg, bg, (((1,), (1,)), ((), ())),
                             preferred_element_type=F32)
        st = state_scr[:, gcols]
        y_off = jnp.dot(cg, st.astype(BF16),
                        preferred_element_type=F32) * ecs_x[:, gcols]
        y_parts = []
        for j in range(SSM_HEADS_PER_GROUP):
            h = g * SSM_HEADS_PER_GROUP + j
            seg = cs2[:, h:h + 1] - cs2_t[h:h + 1, :]
            lmat = jnp.exp2(jnp.where(causal, seg, -jnp.inf))
            mh = (cb * lmat).astype(BF16)
            y_parts.append(jnp.dot(
                mh, xdt_b[:, h * SSM_HEAD_DIM:(h + 1) * SSM_HEAD_DIM],
                preferred_element_type=F32))
        y_g = jnp.concatenate(y_parts, axis=1) + y_off
        upd = lax.dot_general(bg, xdec_b[:, gcols], (((0,), (0,)), ((), ())),
                              preferred_element_type=F32)
        state_scr[:, gcols] = st * ecs_x[q - 1:q, gcols] + upd

        y_g = y_g + xs[:, gcols] * dskip_ref[:, gcols]
        yg = y_g * _half_silu(z_ref[:, gcols].astype(F32))
        ms = jnp.mean(yg * yg, axis=-1, keepdims=True)
        yg = yg * lax.rsqrt(ms + NORM_EPS) * nw_ref[:, gcols]
        o_ref[:, gcols] = yg.astype(BF16)


def _shift_matrix():
    q = SSM_CHUNK
    m = np.zeros(((SSM_CONV - 1) * q, 2 * q), np.float32)
    for k in range(SSM_CONV - 1):
        for t in range(q):
            m[k * q + t, q + t - (SSM_CONV - 1) + k] = 1.0
    return jnp.asarray(m, BF16)


def _ssd(proj3, dt3, cw, cb, dtb, aneg, dskip_x, nw, expand_mat, tri, shift):
    bsz, s, _ = proj3.shape
    q = SSM_CHUNK
    rows = SSD_SUB * q
    const = lambda b, c: (0, 0)
    return pl.pallas_call(
        _ssd_body,
        grid=(bsz, s // rows),
        in_specs=[
            pl.BlockSpec((1, rows, SSM_CONV_DIM), lambda b, c: (b, c, COL_XBC // SSM_CONV_DIM)),
            pl.BlockSpec((1, rows, SSM_INNER), lambda b, c: (b, c, COL_Z // SSM_INNER)),
            pl.BlockSpec((1, rows, LANES), lambda b, c: (b, c, 0)),
            pl.BlockSpec((SSM_CONV, SSM_CONV_DIM), const),
            pl.BlockSpec((1, SSM_CONV_DIM), const),
            pl.BlockSpec((1, LANES), const),
            pl.BlockSpec((1, LANES), const),
            pl.BlockSpec((1, SSM_INNER), const),
            pl.BlockSpec((1, SSM_INNER), const),
            pl.BlockSpec((2 * LANES, SSM_INNER), const),
            pl.BlockSpec((q, q), const),
            pl.BlockSpec(((SSM_CONV - 1) * q, 2 * q), const),
        ],
        out_specs=pl.BlockSpec((1, rows, SSM_INNER), lambda b, c: (b, c, 0)),
        out_shape=jax.ShapeDtypeStruct((bsz, s, SSM_INNER), BF16),
        scratch_shapes=[
            pltpu.VMEM((q, SSM_CONV_DIM), BF16),
            pltpu.VMEM((rows, SSM_INNER), F32),
            pltpu.VMEM((rows, SSM_GROUPS * SSM_STATE), BF16),
            pltpu.VMEM((rows, SSM_GROUPS * SSM_STATE), BF16),
            pltpu.VMEM((SSM_STATE, SSM_INNER), F32),
        ],
        compiler_params=pltpu.CompilerParams(
            dimension_semantics=("arbitrary", "arbitrary"),
            vmem_limit_bytes=VMEM_LIMIT),
        name="ssd",
    )(proj3, proj3, dt3, cw, cb, dtb, aneg, dskip_x, nw, expand_mat, tri, shift)


def _attn_block(q, k, v, bias):
    nq, nk = bias.shape
    low = lax.broadcasted_iota(jnp.int32, (nq, LANES), 1) < ATTN_HEAD_DIM
    q2 = jnp.concatenate([jnp.where(low, q, 0.0), jnp.where(low, 0.0, q)],
                         axis=0).astype(BF16)
    s = lax.dot_general(q2, k.astype(BF16), (((1,), (1,)), ((), ())),
                        preferred_element_type=F32)
    s = s + jnp.concatenate([bias, bias], axis=0)
    mx = jnp.max(s, axis=-1, keepdims=True)
    p = jnp.exp2(s - mx).astype(BF16)
    vext = jnp.concatenate([v.astype(BF16), jnp.ones((nk, LANES), BF16)], axis=1)
    r = jnp.dot(p, vext, preferred_element_type=F32)
    acc = jnp.where(low, r[0:nq, 0:LANES], r[nq:, 0:LANES])
    den = jnp.where(low, r[0:nq, LANES:], r[nq:, LANES:])
    mxp = jnp.where(low, mx[0:nq], mx[nq:])
    return acc / den, mxp + jnp.log2(den)


def _attn_body(qa_ref, qb_ref, kv_ref, out_ref,
               o4_scr, l4_scr, o16_scr, l16_scr, bias_scr, wbias_scr):
    blk = ATTN_BLOCK
    seq = kv_ref.shape[1]
    (w1, d1), (w4, d4), (w16, d16) = ATTN_PATTERNS
    n_back = w1 // d1
    assert n_back == w4 // d4 == w16 // d16 == blk

    qi = lax.broadcasted_iota(jnp.int32, (blk, 2 * blk), 0) + blk
    kj = lax.broadcasted_iota(jnp.int32, (blk, 2 * blk), 1)
    diff = qi - kj
    band = (diff >= 0) & (diff <= n_back)
    bias_scr[0] = jnp.where(band & (kj >= blk), 0.0, -jnp.inf)
    bias_scr[1] = jnp.where(band, 0.0, -jnp.inf)
    wdiff = (lax.broadcasted_iota(jnp.int32, (2 * blk, 2 * blk), 0)
             - lax.broadcasted_iota(jnp.int32, (2 * blk, 2 * blk), 1))
    wbias_scr[...] = jnp.where((wdiff >= 0) & (wdiff <= n_back), 0.0, -jnp.inf)

    def whole_phase(q_ref, unpack_q, o_scr, l_scr, dil):
        length = seq // dil
        assert length == 2 * blk

        def body(r, carry):
            rows = pl.ds(r, length, stride=dil)
            kv = kv_ref[0, rows, :]
            o, lse = _attn_block(unpack_q(q_ref[0, rows, :]), _unpack_high(kv), _unpack_low(kv),
                                 wbias_scr[...])
            o_scr[rows, :] = o
            l_scr[rows, :] = lse
            return carry

        lax.fori_loop(0, dil, body, 0, unroll=ATTN_UNROLL // 2)

    def strided_phase(q_ref, unpack_q, o_scr, l_scr, dil):
        span = blk * dil

        def body(idx, carry):
            r = idx % dil
            n = idx // dil
            cur = r + span * n
            prv = jnp.maximum(cur - span, r)

            def ld(ref, start):
                return ref[0, pl.ds(start, blk, stride=dil), :]

            kv = jnp.concatenate([ld(kv_ref, prv), ld(kv_ref, cur)], axis=0)
            o, lse = _attn_block(unpack_q(ld(q_ref, cur)), _unpack_high(kv), _unpack_low(kv),
                                 bias_scr[jnp.minimum(n, 1)])
            o_scr[pl.ds(cur, blk, stride=dil), :] = o
            l_scr[pl.ds(cur, blk, stride=dil), :] = lse
            return carry

        lax.fori_loop(0, seq // blk, body, 0, unroll=ATTN_UNROLL)

    whole_phase(qb_ref, _unpack_high, o16_scr, l16_scr, d16)
    strided_phase(qa_ref, _unpack_low, o4_scr, l4_scr, d4)

    def body(n, carry):
        cur = pl.multiple_of(n * blk, blk)
        prv = pl.multiple_of(jnp.maximum(cur - blk, 0), blk)
        rows = pl.ds(cur, blk)
        prows = pl.ds(prv, blk)
        kv = jnp.concatenate([kv_ref[0, prows, :], kv_ref[0, rows, :]], axis=0)
        o1, l1 = _attn_block(_unpack_high(qa_ref[0, rows, :]), _unpack_high(kv), _unpack_low(kv),
                             bias_scr[jnp.minimum(n, 1)])
        l4, l16 = l4_scr[rows, :], l16_scr[rows, :]
        lm = jnp.maximum(jnp.maximum(l1, l4), l16)
        e1, e4, e16 = jnp.exp2(l1 - lm), jnp.exp2(l4 - lm), jnp.exp2(l16 - lm)
        y = (e1 * o1 + e4 * o4_scr[rows, :] + e16 * o16_scr[rows, :]) / (e1 + e4 + e16)
        gate = _half_silu(_unpack_low(qb_ref[0, rows, :]))
        out_ref[0, rows, :] = (y * gate).astype(BF16)
        return carry

    lax.fori_loop(0, seq // blk, body, 0, unroll=ATTN_UNROLL)


def _attention(slabs, bsz, seq):
    slab = lambda s0: (lambda b, p: (s0 + p, b, 0))
    spec = lambda s0: pl.BlockSpec((1, seq, LANES), slab(s0))
    return pl.pallas_call(
        _attn_body,
        grid=(bsz, PAIRS),
        in_specs=[spec(SLAB_Q1_Q4), spec(SLAB_Q16_GATE), spec(SLAB_K_V)],
        out_specs=pl.BlockSpec((1, seq, LANES), lambda b, p: (b, 0, p)),
        out_shape=jax.ShapeDtypeStruct((bsz, seq, ATTN_KV_WIDTH), BF16),
        scratch_shapes=[pltpu.VMEM((seq, LANES), F32) for _ in range(4)]
        + [pltpu.VMEM((2, ATTN_BLOCK, 2 * ATTN_BLOCK), F32),
           pltpu.VMEM((2 * ATTN_BLOCK, 2 * ATTN_BLOCK), F32)],
        compiler_params=pltpu.CompilerParams(
            dimension_semantics=("arbitrary", "arbitrary"),
            vmem_limit_bytes=VMEM_LIMIT),
        name="dilated_attention",
    )(slabs, slabs, slabs)


def _merge_body(x_ref, sc_ref, m0_ref, m1_ref, m2_ref, yssm_ref, yattn_ref,
                scw_ref, pssm_ref, pattn_ref, psc_ref, wout_ref, nw_ref,
                out_ref, ext_scr, *, tiles_per_seq):
    tm = x_ref.shape[0]
    halo = SUBLANES
    first = (pl.program_id(0) % tiles_per_seq) == 0

    @pl.when(first)
    def _():
        ext_scr[0:halo, :] = jnp.zeros((halo, SC_WIDTH), F32)

    @pl.when(jnp.logical_not(first))
    def _():
        ext_scr[0:halo, :] = ext_scr[tm:tm + halo, :]

    u = sc_ref[:, 0:SC_WIDTH].astype(F32)
    b_sc = sc_ref[:, SC_WIDTH:2 * SC_WIDTH].astype(F32)
    c_sc = sc_ref[:, 2 * SC_WIDTH:3 * SC_WIDTH].astype(F32)
    g_sc = sc_ref[:, 3 * SC_WIDTH:4 * SC_WIDTH].astype(F32)
    ext_scr[halo:halo + tm, :] = c_sc * u
    conv = jnp.zeros((tm, SC_WIDTH), F32)
    for k in range(SC_CONV):
        lo = halo - (SC_CONV - 1) + k
        conv = conv + scw_ref[k:k + 1, :] * ext_scr[lo:lo + tm, :]
    y_sc = (b_sc * conv * _half_silu(g_sc)).astype(BF16)

    merged = (
        _half_gate2(m0_ref[...].astype(F32))
        * jnp.dot(yssm_ref[...], pssm_ref[...], preferred_element_type=F32)
        + _half_gate2(m1_ref[...].astype(F32))
        * jnp.dot(yattn_ref[...], pattn_ref[...], preferred_element_type=F32)
        + _half_gate2(m2_ref[...].astype(F32))
        * jnp.dot(y_sc, psc_ref[...], preferred_element_type=F32))
    out = jnp.dot(merged.astype(BF16), wout_ref[...], preferred_element_type=F32)
    ms = jnp.mean(out * out, axis=-1, keepdims=True)
    out_ref[...] = x_ref[...] + out * lax.rsqrt(ms + NORM_EPS) * nw_ref[...]


def _merge(x2d, proj, y_ssm, y_attn, scw, p_ssm, p_attn, p_sc, w_out, nw, seq, tm=512):
    m = x2d.shape[0]
    row = lambda c: (lambda i: (i, c))
    const = lambda i: (0, 0)
    wa = ATTN_KV_WIDTH
    return pl.pallas_call(
        functools.partial(_merge_body, tiles_per_seq=seq // tm),
        grid=(m // tm,),
        in_specs=[
            pl.BlockSpec((tm, D_MODEL), row(0)),
            pl.BlockSpec((tm, 4 * SC_WIDTH), row(COL_SC // (4 * SC_WIDTH))),
            pl.BlockSpec((tm, D_MODEL), row(COL_MERGE // D_MODEL)),
            pl.BlockSpec((tm, D_MODEL), row(COL_MERGE // D_MODEL + 1)),
            pl.BlockSpec((tm, D_MODEL), row(COL_MERGE // D_MODEL + 2)),
            pl.BlockSpec((tm, SSM_INNER), row(0)),
            pl.BlockSpec((tm, wa), row(0)),
            pl.BlockSpec((SC_CONV, SC_WIDTH), const),
            pl.BlockSpec((SSM_INNER, D_MODEL), const),
            pl.BlockSpec((wa, D_MODEL), const),
            pl.BlockSpec((SC_WIDTH, D_MODEL), const),
            pl.BlockSpec((D_MODEL, D_MODEL), const),
            pl.BlockSpec((1, D_MODEL), const),
        ],
        out_specs=pl.BlockSpec((tm, D_MODEL), row(0)),
        out_shape=jax.ShapeDtypeStruct((m, D_MODEL), F32),
        scratch_shapes=[pltpu.VMEM((tm + 2 * SUBLANES, SC_WIDTH), F32)],
        compiler_params=pltpu.CompilerParams(
            dimension_semantics=("arbitrary",),
            vmem_limit_bytes=VMEM_LIMIT),
        name="merge_out",
    )(x2d, proj, proj, proj, proj, y_ssm, y_attn,
      scw, p_ssm, p_attn, p_sc, w_out, nw)


def _expand_matrix(n_heads, width):
    m = np.zeros((2 * LANES, n_heads * width), np.float32)
    for h in range(n_heads):
        m[h, h * width:(h + 1) * width] = 1.0
        m[LANES + h, h * width:(h + 1) * width] = 1.0
    return jnp.asarray(m, BF16)


def _pad_lanes(v):
    return jnp.zeros((1, LANES), F32).at[0, :v.shape[0]].set(v.astype(F32))


def _permute_w_in(w):
    offs = np.cumsum([0, SSM_INNER, SSM_CONV_DIM, SSM_HEADS, 3 * ATTN_KV_WIDTH,
                      ATTN_KV_WIDTH, ATTN_KV_WIDTH, ATTN_KV_WIDTH,
                      4 * SC_WIDTH, 3 * D_MODEL])
    z, xbc, dt, q, k, v, gat, sc, mg = [w[:, offs[i]:offs[i + 1]] for i in range(9)]
    q = q * (ATTN_HEAD_DIM ** -0.5 * LOG2E)
    wa = ATTN_KV_WIDTH
    q1, q4, q16 = q[:, :wa], q[:, wa:2 * wa], q[:, 2 * wa:]
    pieces = []
    for hi, lo in ((q1, q4), (q16, 0.5 * gat), (k, v)):
        for p in range(PAIRS):
            cols = slice(p * LANES, (p + 1) * LANES)
            pieces += [hi[:, cols], lo[:, cols]]
    w_qkv = jnp.concatenate(pieces, axis=1).astype(BF16)
    sc = jnp.concatenate([sc[:, :3 * SC_WIDTH], 0.5 * sc[:, 3 * SC_WIDTH:]], axis=1)
    w_perm = jnp.concatenate([xbc, sc, 0.5 * z, 0.5 * mg], axis=1).astype(BF16)
    w_dt = jnp.zeros((D_MODEL, LANES), F32).at[:, :SSM_HEADS].set(dt).astype(BF16)
    return w_qkv, w_perm, w_dt


def _layer(x2d, bsz, seq, norm_pre, norm_post, w_in, ssm_conv_w, ssm_conv_b, dt_bias,
           a_log, d_skip, ssm_norm, sc_conv_w, p_ssm, p_attn, p_sc, w_out,
           expand_ssm, tri, shift):
    w_qkv, w_perm, w_dt = _permute_w_in(w_in)
    nw_pre = norm_pre.reshape(1, D_MODEL)
    slabs = _qkv_proj(x2d, nw_pre, w_qkv)
    proj, dt_raw = _in_proj(x2d, nw_pre, w_perm, w_dt)
    proj3 = proj.reshape(bsz, seq, PROJ_WIDTH)
    dt3 = dt_raw.reshape(bsz, seq, LANES)

    y_ssm = _ssd(
        proj3, dt3, 0.5 * ssm_conv_w, 0.5 * ssm_conv_b.reshape(1, SSM_CONV_DIM),
        _pad_lanes(dt_bias), _pad_lanes(-jnp.exp(a_log.astype(F32))),
        jnp.repeat(d_skip.astype(F32), SSM_HEAD_DIM).reshape(1, SSM_INNER),
        ssm_norm.reshape(1, SSM_INNER), expand_ssm, tri, shift)

    y_attn = _attention(slabs, bsz, seq)

    return _merge(x2d, proj, y_ssm.reshape(bsz * seq, SSM_INNER),
                  y_attn.reshape(bsz * seq, ATTN_KV_WIDTH),
                  sc_conv_w, (0.5 * p_ssm).astype(BF16), (0.5 * p_attn).astype(BF16),
                  (0.5 * p_sc).astype(BF16), w_out.astype(BF16),
                  norm_post.reshape(1, D_MODEL), seq)


def kernel(x, norm_pre, norm_post, w_in, ssm_conv_w, ssm_conv_b, dt_bias, a_log,
           d_skip, ssm_norm, sc_conv_w, p_ssm, p_attn, p_sc, w_out):
    bsz, seq, _ = x.shape
    expand_ssm = _expand_matrix(SSM_HEADS, SSM_HEAD_DIM)
    tri = jnp.asarray(np.tril(np.ones((SSM_CHUNK, SSM_CHUNK), np.float32)))
    shift = _shift_matrix()
    x2d = x.reshape(bsz * seq, D_MODEL)
    for i in range(norm_pre.shape[0]):
        x2d = _layer(x2d, bsz, seq, norm_pre[i], norm_post[i], w_in[i], ssm_conv_w[i],
                     ssm_conv_b[i], dt_bias[i], a_log[i], d_skip[i], ssm_norm[i],
                     sc_conv_w[i], p_ssm[i], p_attn[i], p_sc[i], w_out[i],
                     expand_ssm, tri, shift)
    return x2d.reshape(bsz, seq, D_MODEL)
```

```python
import functools

import jax
import jax.numpy as jnp
import numpy as np
from jax import lax
from jax.experimental import pallas as pl
from jax.experimental.pallas import tpu as pltpu

F32 = jnp.float32
BF16 = jnp.bfloat16

D_MODEL = 1024
SSM_HEADS = 16
SSM_HEAD_DIM = 64
SSM_INNER = SSM_HEADS * SSM_HEAD_DIM
SSM_GROUPS = 4
SSM_HEADS_PER_GROUP = SSM_HEADS // SSM_GROUPS
SSM_STATE = 128
SSM_CONV = 4
SSM_CHUNK = 128
SSM_CONV_DIM = SSM_INNER + 2 * SSM_GROUPS * SSM_STATE
ATTN_HEAD_DIM = 64
ATTN_SLOTS = 8
ATTN_PATTERNS = ((128, 1), (512, 4), (2048, 16))
ATTN_KV_WIDTH = ATTN_SLOTS * ATTN_HEAD_DIM
ATTN_BLOCK = 128
SC_WIDTH = 512
SC_CONV = 3
SSD_SUB = 4
ATTN_UNROLL = 8
NORM_EPS = 1e-6
LOG2E = float(np.log2(np.e))

LANES = 128
SUBLANES = 8
CONV_HALO = 2 * SUBLANES
VMEM_LIMIT = 48 * 1024 * 1024

COL_XBC = 0
COL_SC = COL_XBC + SSM_CONV_DIM
COL_Z = COL_SC + 4 * SC_WIDTH
COL_MERGE = COL_Z + SSM_INNER
PROJ_WIDTH = COL_MERGE + 3 * D_MODEL

PAIRS = ATTN_KV_WIDTH // LANES
SLAB_Q1_Q4 = 0
SLAB_Q16_GATE = SLAB_Q1_Q4 + PAIRS
SLAB_K_V = SLAB_Q16_GATE + PAIRS
N_SLABS = SLAB_K_V + PAIRS
QKV_WIDTH = 2 * N_SLABS * LANES
HIGH_HALF = -65536


def _half_silu(hx):
    return hx * jnp.tanh(hx) + hx


def _half_gate2(hx):
    return jnp.tanh(hx) + 1.0


def _pack_pair(a, b):
    ab = lax.bitcast_convert_type(a.astype(BF16).astype(F32), jnp.int32)
    bb = lax.bitcast_convert_type(b.astype(BF16).astype(F32), jnp.int32)
    return ab | lax.shift_right_logical(bb, 16)


def _unpack_high(w):
    return lax.bitcast_convert_type(w & HIGH_HALF, F32)


def _unpack_low(w):
    return lax.bitcast_convert_type(lax.shift_left(w, 16), F32)


def _split_hi_lo(v):
    hi = v.astype(BF16)
    lo = (v - hi.astype(F32)).astype(BF16)
    return jnp.concatenate([hi, lo], axis=1)


def _normed(x_ref, nw_ref):
    x = x_ref[...]
    ms = jnp.mean(x * x, axis=-1, keepdims=True)
    return (x * lax.rsqrt(ms + NORM_EPS) * nw_ref[...]).astype(BF16)


def _qkvproj_body(x_ref, nw_ref, w_ref, o_ref):
    res = jnp.dot(_normed(x_ref, nw_ref), w_ref[...], preferred_element_type=F32)
    for s in range(N_SLABS):
        hi = res[:, (2 * s) * LANES:(2 * s + 1) * LANES]
        lo = res[:, (2 * s + 1) * LANES:(2 * s + 2) * LANES]
        o_ref[s] = _pack_pair(hi, lo)


def _qkv_proj(x2d, norm_w, w_qkv, tm=512):
    m = x2d.shape[0]
    return pl.pallas_call(
        _qkvproj_body,
        grid=(m // tm,),
        in_specs=[
            pl.BlockSpec((tm, D_MODEL), lambda i: (i, 0)),
            pl.BlockSpec((1, D_MODEL), lambda i: (0, 0)),
            pl.BlockSpec((D_MODEL, QKV_WIDTH), lambda i: (0, 0)),
        ],
        out_specs=pl.BlockSpec((N_SLABS, tm, LANES), lambda i: (0, i, 0)),
        out_shape=jax.ShapeDtypeStruct((N_SLABS, m, LANES), jnp.int32),
        compiler_params=pltpu.CompilerParams(
            dimension_semantics=("arbitrary",),
            vmem_limit_bytes=VMEM_LIMIT),
        name="qkv_proj",
    )(x2d, norm_w, w_qkv)


def _inproj_body(x_ref, nw_ref, w_ref, wdt_ref, o_ref, dt_ref, h_scr):
    @pl.when(pl.program_id(1) == 0)
    def _():
        h = _normed(x_ref, nw_ref)
        h_scr[...] = h
        dt_ref[...] = jnp.dot(h, wdt_ref[...], preferred_element_type=F32)

    o_ref[...] = jnp.dot(h_scr[...], w_ref[...],
                         preferred_element_type=F32).astype(BF16)


def _in_proj(x2d, norm_w, w_perm, w_dt, tm=1024, tn=2048):
    m = x2d.shape[0]
    return pl.pallas_call(
        _inproj_body,
        grid=(m // tm, PROJ_WIDTH // tn),
        in_specs=[
            pl.BlockSpec((tm, D_MODEL), lambda i, j: (i, 0)),
            pl.BlockSpec((1, D_MODEL), lambda i, j: (0, 0)),
            pl.BlockSpec((D_MODEL, tn), lambda i, j: (0, j)),
            pl.BlockSpec((D_MODEL, LANES), lambda i, j: (0, 0)),
        ],
        out_specs=[
            pl.BlockSpec((tm, tn), lambda i, j: (i, j)),
            pl.BlockSpec((tm, LANES), lambda i, j: (i, 0)),
        ],
        out_shape=[
            jax.ShapeDtypeStruct((m, PROJ_WIDTH), BF16),
            jax.ShapeDtypeStruct((m, LANES), F32),
        ],
        scratch_shapes=[pltpu.VMEM((tm, D_MODEL), BF16)],
        compiler_params=pltpu.CompilerParams(
            dimension_semantics=("arbitrary", "arbitrary"),
            vmem_limit_bytes=VMEM_LIMIT),
        name="in_proj",
    )(x2d, norm_w, w_perm, w_dt)


def _ssd_body(xbc_ref, z_ref, dt_ref, cw_ref, cb_ref, dtb_ref, aneg_ref,
              dskip_ref, nw_ref, expand_ref, tri_ref, shift_ref, o_ref,
              prev_scr, xs_scr, b_scr, c_scr, bd_scr, state_scr):
    q = SSM_CHUNK

    @pl.when(pl.program_id(1) == 0)
    def _():
        prev_scr[...] = jnp.zeros_like(prev_scr)
        state_scr[...] = jnp.zeros_like(state_scr)
        bd_scr[...] = jnp.zeros_like(bd_scr)

    row = lax.broadcasted_iota(jnp.int32, (q, q), 0)
    col = lax.broadcasted_iota(jnp.int32, (q, q), 1)
    causal = row >= col

    for sub in range(SSD_SUB):
        rows = slice(sub * q, (sub + 1) * q)
        _ssd_chunk(
            xbc_ref.at[0, rows, :],
            prev_scr if sub == 0 else xbc_ref.at[0, slice((sub - 1) * q, sub * q), :],
            z_ref.at[0, rows, :], dt_ref.at[0, rows, :], cw_ref, cb_ref, dtb_ref, aneg_ref,
            dskip_ref, nw_ref, expand_ref, tri_ref, shift_ref, o_ref.at[0, rows, :],
            xs_scr.at[rows, :], b_scr.at[rows, :], c_scr.at[rows, :], bd_scr.at[sub],
            state_scr, causal)
    prev_scr[...] = xbc_ref[0, (SSD_SUB - 1) * q:SSD_SUB * q, :]


def _ssd_chunk(xbc_ref, prev_ref, z_ref, dt_ref, cw_ref, cb_ref, dtb_ref, aneg_ref,
               dskip_ref, nw_ref, expand_ref, tri_ref, shift_ref, o_ref,
               xs_scr, b_scr, c_scr, bd_scr, state_scr, causal):
    q = SSM_CHUNK

    piece = 512
    for p in range(SSM_CONV_DIM // piece):
        cols = slice(p * piece, (p + 1) * piece)
        win = jnp.concatenate([prev_ref[q - CONV_HALO:q, cols], xbc_ref[:, cols]], axis=0)
        taps = jnp.concatenate([win * cw_ref[k:k + 1, cols] for k in range(SSM_CONV)], axis=0)
        acc = jnp.dot(shift_ref[...], taps, preferred_element_type=F32) + cb_ref[:, cols]
        act = _half_silu(acc)
        if p < 2:
            xs_scr[:, cols] = act
        elif p == 2:
            b_scr[...] = act.astype(BF16)
        else:
            c_scr[...] = act.astype(BF16)

    dt_in = dt_ref[...] + dtb_ref[...]
    dt = jnp.maximum(dt_in, 0.0) + jnp.log(1.0 + jnp.exp(-jnp.abs(dt_in)))
    a = dt * aneg_ref[...]
    a_hi = a.astype(BF16)
    r1 = a - a_hi.astype(F32)
    a_mid = r1.astype(BF16)
    a_lo = (r1 - a_mid.astype(F32)).astype(BF16)
    cs = jnp.dot(tri_ref[...], jnp.concatenate([a_hi, a_mid, a_lo], axis=0),
                 preferred_element_type=F32)
    cs2 = cs * LOG2E
    cs2_t = cs2.T
    ecs = jnp.exp(cs)
    dec = jnp.exp(cs[q - 1:q, :] - cs)
    stacked = jnp.concatenate(
        [_split_hi_lo(dt), _split_hi_lo(ecs), _split_hi_lo(dec)], axis=0)
    expanded = jnp.dot(stacked, expand_ref[...], preferred_element_type=F32)
    dt_x = expanded[0:q]
    ecs_x = expanded[q:2 * q]
    dec_x = expanded[2 * q:3 * q]

    xs = xs_scr[...]
    xdt = xs * dt_x
    xdt_b = xdt.astype(BF16)
    xdec_b = (xdt * dec_x).astype(BF16)

    gw = SSM_HEADS_PER_GROUP * SSM_HEAD_DIM
    for g in range(SSM_GROUPS):
        gcols = slice(g * gw, (g + 1) * gw)
        ncols = slice(g * SSM_STATE, (g + 1) * SSM_STATE)
        bg = b_scr[:, ncols]
        cg = c_scr[:, ncols]
        cb = lax.dot_general(cg, bg, (((1,), (1,)), ((), ())),
                             preferred_element_type=F32)
        st = state_scr[:, gcols]
        y_off = jnp.dot(cg, st.astype(BF16),
                        preferred_element_type=F32) * ecs_x[:, gcols]
        m_parts = []
        for j in range(SSM_HEADS_PER_GROUP):
            h = g * SSM_HEADS_PER_GROUP + j
            seg = cs2[:, h:h + 1] - cs2_t[h:h + 1, :]
            lmat = jnp.exp2(jnp.where(causal, seg, -jnp.inf))
            m_parts.append((cb * lmat).astype(BF16))
            bd_scr[g, j * q:(j + 1) * q, j * SSM_HEAD_DIM:(j + 1) * SSM_HEAD_DIM] = (
                xdt_b[:, h * SSM_HEAD_DIM:(h + 1) * SSM_HEAD_DIM])
        y_g = jnp.dot(jnp.concatenate(m_parts, axis=1), bd_scr[g],
                      preferred_element_type=F32) + y_off
        upd = lax.dot_general(bg, xdec_b[:, gcols], (((0,), (0,)), ((), ())),
                              preferred_element_type=F32)
        state_scr[:, gcols] = st * ecs_x[q - 1:q, gcols] + upd

        y_g = y_g + xs[:, gcols] * dskip_ref[:, gcols]
        yg = y_g * _half_silu(z_ref[:, gcols].astype(F32))
        ms = jnp.mean(yg * yg, axis=-1, keepdims=True)
        yg = yg * lax.rsqrt(ms + NORM_EPS) * nw_ref[:, gcols]
        o_ref[:, gcols] = yg.astype(BF16)


def _shift_matrix():
    q = SSM_CHUNK
    win = q + CONV_HALO
    m = np.zeros((q, SSM_CONV * win), np.float32)
    for k in range(SSM_CONV):
        for t in range(q):
            m[t, k * win + CONV_HALO + t - (SSM_CONV - 1) + k] = 1.0
    return jnp.asarray(m, BF16)


def _cumsum_matrix():
    tri = np.tril(np.ones((SSM_CHUNK, SSM_CHUNK), np.float32))
    return jnp.asarray(np.concatenate([tri, tri, tri], axis=1), BF16)


def _ssd(proj3, dt3, cw, cb, dtb, aneg, dskip_x, nw, expand_mat, tri, shift):
    bsz, s, _ = proj3.shape
    q = SSM_CHUNK
    rows = SSD_SUB * q
    const = lambda b, c: (0, 0)
    return pl.pallas_call(
        _ssd_body,
        grid=(bsz, s // rows),
        in_specs=[
            pl.BlockSpec((1, rows, SSM_CONV_DIM), lambda b, c: (b, c, COL_XBC // SSM_CONV_DIM)),
            pl.BlockSpec((1, rows, SSM_INNER), lambda b, c: (b, c, COL_Z // SSM_INNER)),
            pl.BlockSpec((1, rows, LANES), lambda b, c: (b, c, 0)),
            pl.BlockSpec((SSM_CONV, SSM_CONV_DIM), const),
            pl.BlockSpec((1, SSM_CONV_DIM), const),
            pl.BlockSpec((1, LANES), const),
            pl.BlockSpec((1, LANES), const),
            pl.BlockSpec((1, SSM_INNER), const),
            pl.BlockSpec((1, SSM_INNER), const),
            pl.BlockSpec((2 * LANES, SSM_INNER), const),
            pl.BlockSpec((q, 3 * q), const),
            pl.BlockSpec((q, SSM_CONV * (q + CONV_HALO)), const),
        ],
        out_specs=pl.BlockSpec((1, rows, SSM_INNER), lambda b, c: (b, c, 0)),
        out_shape=jax.ShapeDtypeStruct((bsz, s, SSM_INNER), BF16),
        scratch_shapes=[
            pltpu.VMEM((q, SSM_CONV_DIM), BF16),
            pltpu.VMEM((rows, SSM_INNER), F32),
            pltpu.VMEM((rows, SSM_GROUPS * SSM_STATE), BF16),
            pltpu.VMEM((rows, SSM_GROUPS * SSM_STATE), BF16),
            pltpu.VMEM((SSD_SUB, SSM_GROUPS, SSM_HEADS_PER_GROUP * q,
                        SSM_HEADS_PER_GROUP * SSM_HEAD_DIM), BF16),
            pltpu.VMEM((SSM_STATE, SSM_INNER), F32),
        ],
        compiler_params=pltpu.CompilerParams(
            dimension_semantics=("arbitrary", "arbitrary"),
            vmem_limit_bytes=VMEM_LIMIT),
        name="ssd",
    )(proj3, proj3, dt3, cw, cb, dtb, aneg, dskip_x, nw, expand_mat, tri, shift)


def _attn_block(q, k, v, bias):
    nq, nk = bias.shape
    low = lax.broadcasted_iota(jnp.int32, (nq, LANES), 1) < ATTN_HEAD_DIM
    q2 = jnp.concatenate([jnp.where(low, q, 0.0), jnp.where(low, 0.0, q)],
                         axis=0).astype(BF16)
    s = lax.dot_general(q2, k.astype(BF16), (((1,), (1,)), ((), ())),
                        preferred_element_type=F32)
    s = s + jnp.concatenate([bias, bias], axis=0)
    mx = jnp.max(s, axis=-1, keepdims=True)
    p = jnp.exp2(s - mx).astype(BF16)
    vext = jnp.concatenate([v.astype(BF16), jnp.ones((nk, LANES), BF16)], axis=1)
    r = jnp.dot(p, vext, preferred_element_type=F32)
    acc = jnp.where(low, r[0:nq, 0:LANES], r[nq:, 0:LANES])
    den = jnp.where(low, r[0:nq, LANES:], r[nq:, LANES:])
    mxp = jnp.where(low, mx[0:nq], mx[nq:])
    return acc / den, mxp + jnp.log2(den)


def _attn_body(qa_ref, qb_ref, kv_ref, out_ref,
               o4_scr, l4_scr, o16_scr, l16_scr, bias_scr, wbias_scr):
    blk = ATTN_BLOCK
    seq = kv_ref.shape[1]
    (w1, d1), (w4, d4), (w16, d16) = ATTN_PATTERNS
    n_back = w1 // d1
    assert n_back == w4 // d4 == w16 // d16 == blk

    qi = lax.broadcasted_iota(jnp.int32, (blk, 2 * blk), 0) + blk
    kj = lax.broadcasted_iota(jnp.int32, (blk, 2 * blk), 1)
    diff = qi - kj
    band = (diff >= 0) & (diff <= n_back)
    bias_scr[0] = jnp.where(band & (kj >= blk), 0.0, -jnp.inf)
    bias_scr[1] = jnp.where(band, 0.0, -jnp.inf)
    wdiff = (lax.broadcasted_iota(jnp.int32, (2 * blk, 2 * blk), 0)
             - lax.broadcasted_iota(jnp.int32, (2 * blk, 2 * blk), 1))
    wbias_scr[...] = jnp.where((wdiff >= 0) & (wdiff <= n_back), 0.0, -jnp.inf)

    def whole_phase(q_ref, unpack_q, o_scr, l_scr, dil):
        length = seq // dil
        assert length == 2 * blk

        def body(r, carry):
            rows = pl.ds(r, length, stride=dil)
            kv = kv_ref[0, rows, :]
            o, lse = _attn_block(unpack_q(q_ref[0, rows, :]), _unpack_high(kv), _unpack_low(kv),
                                 wbias_scr[...])
            o_scr[rows, :] = o
            l_scr[rows, :] = lse
            return carry

        lax.fori_loop(0, dil, body, 0, unroll=ATTN_UNROLL // 2)

    def strided_phase(q_ref, unpack_q, o_scr, l_scr, dil):
        span = blk * dil

        def body(idx, carry):
            r = idx % dil
            n = idx // dil
            cur = r + span * n
            prv = jnp.maximum(cur - span, r)

            def ld(ref, start):
                return ref[0, pl.ds(start, blk, stride=dil), :]

            kv = jnp.concatenate([ld(kv_ref, prv), ld(kv_ref, cur)], axis=0)
            o, lse = _attn_block(unpack_q(ld(q_ref, cur)), _unpack_high(kv), _unpack_low(kv),
                                 bias_scr[jnp.minimum(n, 1)])
            o_scr[pl.ds(cur, blk, stride=dil), :] = o
            l_scr[pl.ds(cur, blk, stride=dil), :] = lse
            return carry

        lax.fori_loop(0, seq // blk, body, 0, unroll=ATTN_UNROLL)

    whole_phase(qb_ref, _unpack_high, o16_scr, l16_scr, d16)
    strided_phase(qa_ref, _unpack_low, o4_scr, l4_scr, d4)

    def body(n, carry):
        cur = pl.multiple_of(n * blk, blk)
        prv = pl.multiple_of(jnp.maximum(cur - blk, 0), blk)
        rows = pl.ds(cur, blk)
        prows = pl.ds(prv, blk)
        kv = jnp.concatenate([kv_ref[0, prows, :], kv_ref[0, rows, :]], axis=0)
        o1, l1 = _attn_block(_unpack_high(qa_ref[0, rows, :]), _unpack_high(kv), _unpack_low(kv),
                             bias_scr[jnp.minimum(n, 1)])
        l4, l16 = l4_scr[rows, :], l16_scr[rows, :]
        lm = jnp.maximum(jnp.maximum(l1, l4), l16)
        e1, e4, e16 = jnp.exp2(l1 - lm), jnp.exp2(l4 - lm), jnp.exp2(l16 - lm)
        y = (e1 * o1 + e4 * o4_scr[rows, :] + e16 * o16_scr[rows, :]) / (e1 + e4 + e16)
        gate = _half_silu(_unpack_low(qb_ref[0, rows, :]))
        out_ref[0, rows, :] = (y * gate).astype(BF16)
        return carry

    lax.fori_loop(0, seq // blk, body, 0, unroll=ATTN_UNROLL)


def _attention(slabs, bsz, seq):
    slab = lambda s0: (lambda b, p: (s0 + p, b, 0))
    spec = lambda s0: pl.BlockSpec((1, seq, LANES), slab(s0))
    return pl.pallas_call(
        _attn_body,
        grid=(bsz, PAIRS),
        in_specs=[spec(SLAB_Q1_Q4), spec(SLAB_Q16_GATE), spec(SLAB_K_V)],
        out_specs=pl.BlockSpec((1, seq, LANES), lambda b, p: (b, 0, p)),
        out_shape=jax.ShapeDtypeStruct((bsz, seq, ATTN_KV_WIDTH), BF16),
        scratch_shapes=[pltpu.VMEM((seq, LANES), F32) for _ in range(4)]
        + [pltpu.VMEM((2, ATTN_BLOCK, 2 * ATTN_BLOCK), F32),
           pltpu.VMEM((2 * ATTN_BLOCK, 2 * ATTN_BLOCK), F32)],
        compiler_params=pltpu.CompilerParams(
            dimension_semantics=("arbitrary", "arbitrary"),
            vmem_limit_bytes=VMEM_LIMIT),
        name="dilated_attention",
    )(slabs, slabs, slabs)


def _merge_body(x_ref, sc_ref, m0_ref, m1_ref, m2_ref, yssm_ref, yattn_ref,
                scw_ref, pssm_ref, pattn_ref, psc_ref, wout_ref, nw_ref,
                out_ref, ext_scr, *, tiles_per_seq):
    tm = x_ref.shape[0]
    halo = SUBLANES
    first = (pl.program_id(0) % tiles_per_seq) == 0

    @pl.when(first)
    def _():
        ext_scr[0:halo, :] = jnp.zeros((halo, SC_WIDTH), F32)

    @pl.when(jnp.logical_not(first))
    def _():
        ext_scr[0:halo, :] = ext_scr[tm:tm + halo, :]

    u = sc_ref[:, 0:SC_WIDTH].astype(F32)
    b_sc = sc_ref[:, SC_WIDTH:2 * SC_WIDTH].astype(F32)
    c_sc = sc_ref[:, 2 * SC_WIDTH:3 * SC_WIDTH].astype(F32)
    g_sc = sc_ref[:, 3 * SC_WIDTH:4 * SC_WIDTH].astype(F32)
    ext_scr[halo:halo + tm, :] = c_sc * u
    conv = jnp.zeros((tm, SC_WIDTH), F32)
    for k in range(SC_CONV):
        lo = halo - (SC_CONV - 1) + k
        conv = conv + scw_ref[k:k + 1, :] * ext_scr[lo:lo + tm, :]
    y_sc = (b_sc * conv * _half_silu(g_sc)).astype(BF16)

    merged = (
        _half_gate2(m0_ref[...].astype(F32))
        * jnp.dot(yssm_ref[...], pssm_ref[...], preferred_element_type=F32)
        + _half_gate2(m1_ref[...].astype(F32))
        * jnp.dot(yattn_ref[...], pattn_ref[...], preferred_element_type=F32)
        + _half_gate2(m2_ref[...].astype(F32))
        * jnp.dot(y_sc, psc_ref[...], preferred_element_type=F32))
    out = jnp.dot(merged.astype(BF16), wout_ref[...], preferred_element_type=F32)
    ms = jnp.mean(out * out, axis=-1, keepdims=True)
    out_ref[...] = x_ref[...] + out * lax.rsqrt(ms + NORM_EPS) * nw_ref[...]


def _merge(x2d, proj, y_ssm, y_attn, scw, p_ssm, p_attn, p_sc, w_out, nw, seq, tm=512):
    m = x2d.shape[0]
    row = lambda c: (lambda i: (i, c))
    const = lambda i: (0, 0)
    wa = ATTN_KV_WIDTH
    return pl.pallas_call(
        functools.partial(_merge_body, tiles_per_seq=seq // tm),
        grid=(m // tm,),
        in_specs=[
            pl.BlockSpec((tm, D_MODEL), row(0)),
            pl.BlockSpec((tm, 4 * SC_WIDTH), row(COL_SC // (4 * SC_WIDTH))),
            pl.BlockSpec((tm, D_MODEL), row(COL_MERGE // D_MODEL)),
            pl.BlockSpec((tm, D_MODEL), row(COL_MERGE // D_MODEL + 1)),
            pl.BlockSpec((tm, D_MODEL), row(COL_MERGE // D_MODEL + 2)),
            pl.BlockSpec((tm, SSM_INNER), row(0)),
            pl.BlockSpec((tm, wa), row(0)),
            pl.BlockSpec((SC_CONV, SC_WIDTH), const),
            pl.BlockSpec((SSM_INNER, D_MODEL), const),
            pl.BlockSpec((wa, D_MODEL), const),
            pl.BlockSpec((SC_WIDTH, D_MODEL), const),
            pl.BlockSpec((D_MODEL, D_MODEL), const),
            pl.BlockSpec((1, D_MODEL), const),
        ],
        out_specs=pl.BlockSpec((tm, D_MODEL), row(0)),
        out_shape=jax.ShapeDtypeStruct((m, D_MODEL), F32),
        scratch_shapes=[pltpu.VMEM((tm + 2 * SUBLANES, SC_WIDTH), F32)],
        compiler_params=pltpu.CompilerParams(
            dimension_semantics=("arbitrary",),
            vmem_limit_bytes=VMEM_LIMIT),
        name="merge_out",
    )(x2d, proj, proj, proj, proj, y_ssm, y_attn,
      scw, p_ssm, p_attn, p_sc, w_out, nw)


def _expand_matrix(n_heads, width):
    m = np.zeros((2 * LANES, n_heads * width), np.float32)
    for h in range(n_heads):
        m[h, h * width:(h + 1) * width] = 1.0
        m[LANES + h, h * width:(h + 1) * width] = 1.0
    return jnp.asarray(m, BF16)


def _pad_lanes(v):
    return jnp.pad(v.astype(F32), ((0, 0), (0, LANES - v.shape[1])))[:, None, :]


def _interleave_pairs(hi, lo):
    depth = hi.shape[0]
    h = hi.reshape(depth, D_MODEL, PAIRS, LANES)
    l = lo.reshape(depth, D_MODEL, PAIRS, LANES)
    return jnp.stack([h, l], axis=3).reshape(depth, D_MODEL, 2 * PAIRS * LANES)


def _prepare_w_in(w):
    offs = np.cumsum([0, SSM_INNER, SSM_CONV_DIM, SSM_HEADS, 3 * ATTN_KV_WIDTH,
                      ATTN_KV_WIDTH, ATTN_KV_WIDTH, ATTN_KV_WIDTH,
                      4 * SC_WIDTH, 3 * D_MODEL])
    z, xbc, dt, q, k, v, gat, sc, mg = [w[..., offs[i]:offs[i + 1]] for i in range(9)]
    q = q * (ATTN_HEAD_DIM ** -0.5 * LOG2E)
    wa = ATTN_KV_WIDTH
    w_qkv = jnp.concatenate(
        [_interleave_pairs(q[..., :wa], q[..., wa:2 * wa]),
         _interleave_pairs(q[..., 2 * wa:], 0.5 * gat),
         _interleave_pairs(k, v)], axis=-1).astype(BF16)
    w_perm = jnp.concatenate(
        [xbc, sc[..., :3 * SC_WIDTH], 0.5 * sc[..., 3 * SC_WIDTH:], 0.5 * z, 0.5 * mg],
        axis=-1).astype(BF16)
    w_dt = jnp.pad(dt, ((0, 0), (0, 0), (0, LANES - SSM_HEADS))).astype(BF16)
    return w_qkv, w_perm, w_dt


def kernel(x, norm_pre, norm_post, w_in, ssm_conv_w, ssm_conv_b, dt_bias, a_log,
           d_skip, ssm_norm, sc_conv_w, p_ssm, p_attn, p_sc, w_out):
    bsz, seq, _ = x.shape
    depth = norm_pre.shape[0]
    expand_ssm = _expand_matrix(SSM_HEADS, SSM_HEAD_DIM)
    cumsum_mat = _cumsum_matrix()
    shift = _shift_matrix()

    w_qkv, w_perm, w_dt = _prepare_w_in(w_in)
    nw_pre = norm_pre.reshape(depth, 1, D_MODEL)
    nw_post = norm_post.reshape(depth, 1, D_MODEL)
    conv_w = (0.5 * ssm_conv_w).astype(BF16)
    conv_b = (0.5 * ssm_conv_b).reshape(depth, 1, SSM_CONV_DIM)
    dtb = _pad_lanes(dt_bias)
    a_neg = _pad_lanes(-jnp.exp(a_log.astype(F32)))
    dskip_x = jnp.repeat(d_skip.astype(F32), SSM_HEAD_DIM, axis=1).reshape(depth, 1, SSM_INNER)
    ssm_nw = ssm_norm.reshape(depth, 1, SSM_INNER)
    p_ssm_h, p_attn_h, p_sc_h = [(0.5 * p).astype(BF16) for p in (p_ssm, p_attn, p_sc)]
    w_out_b = w_out.astype(BF16)

    x2d = x.reshape(bsz * seq, D_MODEL)
    for i in range(depth):
        slabs = _qkv_proj(x2d, nw_pre[i], w_qkv[i])
        proj, dt_raw = _in_proj(x2d, nw_pre[i], w_perm[i], w_dt[i])
        y_ssm = _ssd(proj.reshape(bsz, seq, PROJ_WIDTH), dt_raw.reshape(bsz, seq, LANES),
                     conv_w[i], conv_b[i], dtb[i], a_neg[i], dskip_x[i], ssm_nw[i],
                     expand_ssm, cumsum_mat, shift)
        y_attn = _attention(slabs, bsz, seq)
        x2d = _merge(x2d, proj, y_ssm.reshape(bsz * seq, SSM_INNER),
                     y_attn.reshape(bsz * seq, ATTN_KV_WIDTH),
                     sc_conv_w[i], p_ssm_h[i], p_attn_h[i], p_sc_h[i], w_out_b[i],
                     nw_post[i], seq)
    return x2d.reshape(bsz, seq, D_MODEL)
```

```python
import functools

import jax
import jax.numpy as jnp
import numpy as np
from jax import lax
from jax.experimental import pallas as pl
from jax.experimental.pallas import tpu as pltpu

F32 = jnp.float32
BF16 = jnp.bfloat16

D_MODEL = 1024
SSM_HEADS = 16
SSM_HEAD_DIM = 64
SSM_INNER = SSM_HEADS * SSM_HEAD_DIM
SSM_GROUPS = 4
SSM_HEADS_PER_GROUP = SSM_HEADS // SSM_GROUPS
SSM_STATE = 128
SSM_CONV = 4
SSM_CHUNK = 128
SSM_CONV_DIM = SSM_INNER + 2 * SSM_GROUPS * SSM_STATE
ATTN_HEAD_DIM = 64
ATTN_SLOTS = 8
ATTN_PATTERNS = ((128, 1), (512, 4), (2048, 16))
ATTN_KV_WIDTH = ATTN_SLOTS * ATTN_HEAD_DIM
ATTN_BLOCK = 128
SC_WIDTH = 512
SC_CONV = 3
SSD_SUB = 4
ATTN_UNROLL = 8
NORM_EPS = 1e-6
LOG2E = float(np.log2(np.e))

LANES = 128
SUBLANES = 8
CONV_HALO = 2 * SUBLANES
VMEM_LIMIT = 48 * 1024 * 1024

COL_XBC = 0
COL_SC = COL_XBC + SSM_CONV_DIM
COL_Z = COL_SC + 4 * SC_WIDTH
COL_MERGE = COL_Z + SSM_INNER
PROJ_WIDTH = COL_MERGE + 3 * D_MODEL

PAIRS = ATTN_KV_WIDTH // LANES
SLAB_Q1_Q4 = 0
SLAB_Q16_GATE = SLAB_Q1_Q4 + PAIRS
SLAB_K_V = SLAB_Q16_GATE + PAIRS
N_SLABS = SLAB_K_V + PAIRS
QKV_WIDTH = 2 * N_SLABS * LANES
QCOL_Q1 = 0
QCOL_Q4 = QCOL_Q1 + ATTN_KV_WIDTH
QCOL_Q16 = QCOL_Q4 + ATTN_KV_WIDTH
QCOL_K = QCOL_Q16 + ATTN_KV_WIDTH
QCOL_V = QCOL_K + ATTN_KV_WIDTH
QCOL_GATE = QCOL_V + ATTN_KV_WIDTH
HIGH_HALF = -65536


def _half_silu(hx):
    return hx * jnp.tanh(hx) + hx


def _half_gate2(hx):
    return jnp.tanh(hx) + 1.0


def _pack_pair(a, b):
    ab = lax.bitcast_convert_type(a.astype(BF16).astype(F32), jnp.int32)
    bb = lax.bitcast_convert_type(b.astype(BF16).astype(F32), jnp.int32)
    return ab | lax.shift_right_logical(bb, 16)


def _unpack_high(w):
    return lax.bitcast_convert_type(w & HIGH_HALF, F32)


def _unpack_low(w):
    return lax.bitcast_convert_type(lax.shift_left(w, 16), F32)


def _split_hi_lo(v):
    hi = v.astype(BF16)
    lo = (v - hi.astype(F32)).astype(BF16)
    return jnp.concatenate([hi, lo], axis=1)


def _normed(x_ref, nw_ref):
    x = x_ref[0]
    ms = jnp.mean(x * x, axis=-1, keepdims=True)
    return (x * lax.rsqrt(ms + NORM_EPS) * nw_ref[...]).astype(BF16)


def _qkvproj_body(x_ref, nw_ref, w_ref, o_ref):
    res = jnp.dot(_normed(x_ref, nw_ref), w_ref[...], preferred_element_type=F32)
    for first, (hi_col, lo_col) in ((SLAB_Q1_Q4, (QCOL_Q1, QCOL_Q4)),
                                    (SLAB_Q16_GATE, (QCOL_Q16, QCOL_GATE)),
                                    (SLAB_K_V, (QCOL_K, QCOL_V))):
        for p in range(PAIRS):
            hi = res[:, hi_col + p * LANES:hi_col + (p + 1) * LANES]
            lo = res[:, lo_col + p * LANES:lo_col + (p + 1) * LANES]
            o_ref[first + p] = _pack_pair(hi, lo)


def _qkv_proj(x, norm_w, w_qkv, tm=512):
    bsz, seq, _ = x.shape
    tps = seq // tm
    return pl.pallas_call(
        _qkvproj_body,
        grid=(bsz * tps,),
        in_specs=[
            pl.BlockSpec((1, tm, D_MODEL), lambda i: (i // tps, i % tps, 0)),
            pl.BlockSpec((1, D_MODEL), lambda i: (0, 0)),
            pl.BlockSpec((D_MODEL, QKV_WIDTH), lambda i: (0, 0)),
        ],
        out_specs=pl.BlockSpec((N_SLABS, tm, LANES), lambda i: (0, i, 0)),
        out_shape=jax.ShapeDtypeStruct((N_SLABS, bsz * seq, LANES), jnp.int32),
        compiler_params=pltpu.CompilerParams(
            dimension_semantics=("arbitrary",),
            vmem_limit_bytes=VMEM_LIMIT),
        name="qkv_proj",
    )(x, norm_w, w_qkv)


def _inproj_body(x_ref, nw_ref, w_ref, wdt_ref, o_ref, dt_ref, h_scr):
    @pl.when(pl.program_id(1) == 0)
    def _():
        h = _normed(x_ref, nw_ref)
        h_scr[...] = h
        dt_ref[0] = jnp.dot(h, wdt_ref[...], preferred_element_type=F32)

    o_ref[0] = jnp.dot(h_scr[...], w_ref[...], preferred_element_type=F32).astype(BF16)


def _in_proj(x, norm_w, w_perm, w_dt, tm=1024, tn=2048):
    bsz, seq, _ = x.shape
    tps = seq // tm
    return pl.pallas_call(
        _inproj_body,
        grid=(bsz * tps, PROJ_WIDTH // tn),
        in_specs=[
            pl.BlockSpec((1, tm, D_MODEL), lambda i, j: (i // tps, i % tps, 0)),
            pl.BlockSpec((1, D_MODEL), lambda i, j: (0, 0)),
            pl.BlockSpec((D_MODEL, tn), lambda i, j: (0, j)),
            pl.BlockSpec((D_MODEL, LANES), lambda i, j: (0, 0)),
        ],
        out_specs=[
            pl.BlockSpec((1, tm, tn), lambda i, j: (i // tps, i % tps, j)),
            pl.BlockSpec((1, tm, LANES), lambda i, j: (i // tps, i % tps, 0)),
        ],
        out_shape=[
            jax.ShapeDtypeStruct((bsz, seq, PROJ_WIDTH), BF16),
            jax.ShapeDtypeStruct((bsz, seq, LANES), F32),
        ],
        scratch_shapes=[pltpu.VMEM((tm, D_MODEL), BF16)],
        compiler_params=pltpu.CompilerParams(
            dimension_semantics=("arbitrary", "arbitrary"),
            vmem_limit_bytes=VMEM_LIMIT),
        name="in_proj",
    )(x, norm_w, w_perm, w_dt)


def _ssd_body(xbc_ref, z_ref, dt_ref, cw_ref, cb_ref, dtb_ref, aneg_ref,
              dskip_ref, nw_ref, expand_ref, tri_ref, shift_ref, o_ref,
              prev_scr, xs_scr, b_scr, c_scr, bd_scr, state_scr):
    q = SSM_CHUNK

    @pl.when(pl.program_id(1) == 0)
    def _():
        prev_scr[...] = jnp.zeros_like(prev_scr)
        state_scr[...] = jnp.zeros_like(state_scr)
        bd_scr[...] = jnp.zeros_like(bd_scr)

    row = lax.broadcasted_iota(jnp.int32, (q, q), 0)
    col = lax.broadcasted_iota(jnp.int32, (q, q), 1)
    causal = row >= col

    for sub in range(SSD_SUB):
        rows = slice(sub * q, (sub + 1) * q)
        _ssd_chunk(
            xbc_ref.at[0, rows, :],
            prev_scr if sub == 0 else xbc_ref.at[0, slice((sub - 1) * q, sub * q), :],
            z_ref.at[0, rows, :], dt_ref.at[0, rows, :], cw_ref, cb_ref, dtb_ref, aneg_ref,
            dskip_ref, nw_ref, expand_ref, tri_ref, shift_ref, o_ref.at[0, rows, :],
            xs_scr.at[rows, :], b_scr.at[rows, :], c_scr.at[rows, :], bd_scr.at[sub],
            state_scr, causal)
    prev_scr[...] = xbc_ref[0, (SSD_SUB - 1) * q:SSD_SUB * q, :]


def _ssd_chunk(xbc_ref, prev_ref, z_ref, dt_ref, cw_ref, cb_ref, dtb_ref, aneg_ref,
               dskip_ref, nw_ref, expand_ref, tri_ref, shift_ref, o_ref,
               xs_scr, b_scr, c_scr, bd_scr, state_scr, causal):
    q = SSM_CHUNK

    piece = 512
    for p in range(SSM_CONV_DIM // piece):
        cols = slice(p * piece, (p + 1) * piece)
        win = jnp.concatenate([prev_ref[q - CONV_HALO:q, cols], xbc_ref[:, cols]], axis=0)
        taps = jnp.concatenate([win * cw_ref[k:k + 1, cols] for k in range(SSM_CONV)], axis=0)
        acc = jnp.dot(shift_ref[...], taps, preferred_element_type=F32) + cb_ref[:, cols]
        act = _half_silu(acc)
        if p < 2:
            xs_scr[:, cols] = act
        elif p == 2:
            b_scr[...] = act.astype(BF16)
        else:
            c_scr[...] = act.astype(BF16)

    dt_in = dt_ref[...] + dtb_ref[...]
    dt = jnp.maximum(dt_in, 0.0) + jnp.log(1.0 + jnp.exp(-jnp.abs(dt_in)))
    a = dt * aneg_ref[...]
    a_hi = a.astype(BF16)
    r1 = a - a_hi.astype(F32)
    a_mid = r1.astype(BF16)
    a_lo = (r1 - a_mid.astype(F32)).astype(BF16)
    cs = jnp.dot(tri_ref[...], jnp.concatenate([a_hi, a_mid, a_lo], axis=0),
                 preferred_element_type=F32)
    cs2 = cs * LOG2E
    cs2_t = cs2.T
    ecs = jnp.exp(cs)
    dec = jnp.exp(cs[q - 1:q, :] - cs)
    stacked = jnp.concatenate(
        [_split_hi_lo(dt), _split_hi_lo(ecs), _split_hi_lo(dec)], axis=0)
    expanded = jnp.dot(stacked, expand_ref[...], preferred_element_type=F32)
    dt_x = expanded[0:q]
    ecs_x = expanded[q:2 * q]
    dec_x = expanded[2 * q:3 * q]

    xs = xs_scr[...]
    xdt = xs * dt_x
    xdt_b = xdt.astype(BF16)
    xdec_b = (xdt * dec_x).astype(BF16)

    gw = SSM_HEADS_PER_GROUP * SSM_HEAD_DIM
    for g in range(SSM_GROUPS):
        gcols = slice(g * gw, (g + 1) * gw)
        ncols = slice(g * SSM_STATE, (g + 1) * SSM_STATE)
        bg = b_scr[:, ncols]
        cg = c_scr[:, ncols]
        cb = lax.dot_general(cg, bg, (((1,), (1,)), ((), ())),
                             preferred_element_type=F32)
        st = state_scr[:, gcols]
        y_off = jnp.dot(cg, st.astype(BF16),
                        preferred_element_type=F32) * ecs_x[:, gcols]
        m_parts = []
        for j in range(SSM_HEADS_PER_GROUP):
            h = g * SSM_HEADS_PER_GROUP + j
            seg = cs2[:, h:h + 1] - cs2_t[h:h + 1, :]
            lmat = jnp.exp2(jnp.where(causal, seg, -jnp.inf))
            m_parts.append((cb * lmat).astype(BF16))
            bd_scr[g, j * q:(j + 1) * q, j * SSM_HEAD_DIM:(j + 1) * SSM_HEAD_DIM] = (
                xdt_b[:, h * SSM_HEAD_DIM:(h + 1) * SSM_HEAD_DIM])
        y_g = jnp.dot(jnp.concatenate(m_parts, axis=1), bd_scr[g],
                      preferred_element_type=F32) + y_off
        upd = lax.dot_general(bg, xdec_b[:, gcols], (((0,), (0,)), ((), ())),
                              preferred_element_type=F32)
        state_scr[:, gcols] = st * ecs_x[q - 1:q, gcols] + upd

        y_g = y_g + xs[:, gcols] * dskip_ref[:, gcols]
        yg = y_g * _half_silu(z_ref[:, gcols].astype(F32))
        ms = jnp.mean(yg * yg, axis=-1, keepdims=True)
        yg = yg * lax.rsqrt(ms + NORM_EPS) * nw_ref[:, gcols]
        o_ref[:, gcols] = yg.astype(BF16)


def _shift_matrix():
    q = SSM_CHUNK
    win = q + CONV_HALO
    m = np.zeros((q, SSM_CONV * win), np.float32)
    for k in range(SSM_CONV):
        for t in range(q):
            m[t, k * win + CONV_HALO + t - (SSM_CONV - 1) + k] = 1.0
    return jnp.asarray(m, BF16)


def _cumsum_matrix():
    tri = np.tril(np.ones((SSM_CHUNK, SSM_CHUNK), np.float32))
    return jnp.asarray(np.concatenate([tri, tri, tri], axis=1), BF16)


def _ssd(proj3, dt3, cw, cb, dtb, aneg, dskip_x, nw, expand_mat, tri, shift):
    bsz, s, _ = proj3.shape
    q = SSM_CHUNK
    rows = SSD_SUB * q
    const = lambda b, c: (0, 0)
    return pl.pallas_call(
        _ssd_body,
        grid=(bsz, s // rows),
        in_specs=[
            pl.BlockSpec((1, rows, SSM_CONV_DIM), lambda b, c: (b, c, COL_XBC // SSM_CONV_DIM)),
            pl.BlockSpec((1, rows, SSM_INNER), lambda b, c: (b, c, COL_Z // SSM_INNER)),
            pl.BlockSpec((1, rows, LANES), lambda b, c: (b, c, 0)),
            pl.BlockSpec((SSM_CONV, SSM_CONV_DIM), const),
            pl.BlockSpec((1, SSM_CONV_DIM), const),
            pl.BlockSpec((1, LANES), const),
            pl.BlockSpec((1, LANES), const),
            pl.BlockSpec((1, SSM_INNER), const),
            pl.BlockSpec((1, SSM_INNER), const),
            pl.BlockSpec((2 * LANES, SSM_INNER), const),
            pl.BlockSpec((q, 3 * q), const),
            pl.BlockSpec((q, SSM_CONV * (q + CONV_HALO)), const),
        ],
        out_specs=pl.BlockSpec((1, rows, SSM_INNER), lambda b, c: (b, c, 0)),
        out_shape=jax.ShapeDtypeStruct((bsz, s, SSM_INNER), BF16),
        scratch_shapes=[
            pltpu.VMEM((q, SSM_CONV_DIM), BF16),
            pltpu.VMEM((rows, SSM_INNER), F32),
            pltpu.VMEM((rows, SSM_GROUPS * SSM_STATE), BF16),
            pltpu.VMEM((rows, SSM_GROUPS * SSM_STATE), BF16),
            pltpu.VMEM((SSD_SUB, SSM_GROUPS, SSM_HEADS_PER_GROUP * q,
                        SSM_HEADS_PER_GROUP * SSM_HEAD_DIM), BF16),
            pltpu.VMEM((SSM_STATE, SSM_INNER), F32),
        ],
        compiler_params=pltpu.CompilerParams(
            dimension_semantics=("arbitrary", "arbitrary"),
            vmem_limit_bytes=VMEM_LIMIT),
        name="ssd",
    )(proj3, proj3, dt3, cw, cb, dtb, aneg, dskip_x, nw, expand_mat, tri, shift)


def _attn_block(q, k, v, bias):
    nq, nk = bias.shape
    low = lax.broadcasted_iota(jnp.int32, (nq, LANES), 1) < ATTN_HEAD_DIM
    q2 = jnp.concatenate([jnp.where(low, q, 0.0), jnp.where(low, 0.0, q)],
                         axis=0).astype(BF16)
    s = lax.dot_general(q2, k.astype(BF16), (((1,), (1,)), ((), ())),
                        preferred_element_type=F32)
    s = s + jnp.concatenate([bias, bias], axis=0)
    mx = jnp.max(s, axis=-1, keepdims=True)
    p = jnp.exp2(s - mx).astype(BF16)
    vext = jnp.concatenate([v.astype(BF16), jnp.ones((nk, LANES), BF16)], axis=1)
    r = jnp.dot(p, vext, preferred_element_type=F32)
    acc = jnp.where(low, r[0:nq, 0:LANES], r[nq:, 0:LANES])
    den = jnp.where(low, r[0:nq, LANES:], r[nq:, LANES:])
    mxp = jnp.where(low, mx[0:nq], mx[nq:])
    return acc / den, mxp + jnp.log2(den)


def _attn_body(qa_ref, qb_ref, kv_ref, out_ref,
               o4_scr, l4_scr, o16_scr, l16_scr, bias_scr, wbias_scr):
    blk = ATTN_BLOCK
    seq = kv_ref.shape[1]
    (w1, d1), (w4, d4), (w16, d16) = ATTN_PATTERNS
    n_back = w1 // d1
    assert n_back == w4 // d4 == w16 // d16 == blk

    qi = lax.broadcasted_iota(jnp.int32, (blk, 2 * blk), 0) + blk
    kj = lax.broadcasted_iota(jnp.int32, (blk, 2 * blk), 1)
    diff = qi - kj
    band = (diff >= 0) & (diff <= n_back)
    bias_scr[0] = jnp.where(band & (kj >= blk), 0.0, -jnp.inf)
    bias_scr[1] = jnp.where(band, 0.0, -jnp.inf)
    wdiff = (lax.broadcasted_iota(jnp.int32, (2 * blk, 2 * blk), 0)
             - lax.broadcasted_iota(jnp.int32, (2 * blk, 2 * blk), 1))
    wbias_scr[...] = jnp.where((wdiff >= 0) & (wdiff <= n_back), 0.0, -jnp.inf)

    def whole_phase(q_ref, unpack_q, o_scr, l_scr, dil):
        length = seq // dil
        assert length == 2 * blk

        def body(r, carry):
            rows = pl.ds(r, length, stride=dil)
            kv = kv_ref[0, rows, :]
            o, lse = _attn_block(unpack_q(q_ref[0, rows, :]), _unpack_high(kv), _unpack_low(kv),
                                 wbias_scr[...])
            o_scr[rows, :] = o
            l_scr[rows, :] = lse
            return carry

        lax.fori_loop(0, dil, body, 0, unroll=ATTN_UNROLL // 2)

    def strided_phase(q_ref, unpack_q, o_scr, l_scr, dil):
        span = blk * dil

        def body(idx, carry):
            r = idx % dil
            n = idx // dil
            cur = r + span * n
            prv = jnp.maximum(cur - span, r)

            def ld(ref, start):
                return ref[0, pl.ds(start, blk, stride=dil), :]

            kv = jnp.concatenate([ld(kv_ref, prv), ld(kv_ref, cur)], axis=0)
            o, lse = _attn_block(unpack_q(ld(q_ref, cur)), _unpack_high(kv), _unpack_low(kv),
                                 bias_scr[jnp.minimum(n, 1)])
            o_scr[pl.ds(cur, blk, stride=dil), :] = o
            l_scr[pl.ds(cur, blk, stride=dil), :] = lse
            return carry

        lax.fori_loop(0, seq // blk, body, 0, unroll=ATTN_UNROLL)

    whole_phase(qb_ref, _unpack_high, o16_scr, l16_scr, d16)
    strided_phase(qa_ref, _unpack_low, o4_scr, l4_scr, d4)

    def body(n, carry):
        cur = pl.multiple_of(n * blk, blk)
        prv = pl.multiple_of(jnp.maximum(cur - blk, 0), blk)
        rows = pl.ds(cur, blk)
        prows = pl.ds(prv, blk)
        kv = jnp.concatenate([kv_ref[0, prows, :], kv_ref[0, rows, :]], axis=0)
        o1, l1 = _attn_block(_unpack_high(qa_ref[0, rows, :]), _unpack_high(kv), _unpack_low(kv),
                             bias_scr[jnp.minimum(n, 1)])
        l4, l16 = l4_scr[rows, :], l16_scr[rows, :]
        lm = jnp.maximum(jnp.maximum(l1, l4), l16)
        e1, e4, e16 = jnp.exp2(l1 - lm), jnp.exp2(l4 - lm), jnp.exp2(l16 - lm)
        y = (e1 * o1 + e4 * o4_scr[rows, :] + e16 * o16_scr[rows, :]) / (e1 + e4 + e16)
        gate = _half_silu(_unpack_low(qb_ref[0, rows, :]))
        out_ref[0, rows, :] = (y * gate).astype(BF16)
        return carry

    lax.fori_loop(0, seq // blk, body, 0, unroll=ATTN_UNROLL)


def _attention(slabs, bsz, seq):
    slab = lambda s0: (lambda b, p: (s0 + p, b, 0))
    spec = lambda s0: pl.BlockSpec((1, seq, LANES), slab(s0))
    return pl.pallas_call(
        _attn_body,
        grid=(bsz, PAIRS),
        in_specs=[spec(SLAB_Q1_Q4), spec(SLAB_Q16_GATE), spec(SLAB_K_V)],
        out_specs=pl.BlockSpec((1, seq, LANES), lambda b, p: (b, 0, p)),
        out_shape=jax.ShapeDtypeStruct((bsz, seq, ATTN_KV_WIDTH), BF16),
        scratch_shapes=[pltpu.VMEM((seq, LANES), F32) for _ in range(4)]
        + [pltpu.VMEM((2, ATTN_BLOCK, 2 * ATTN_BLOCK), F32),
           pltpu.VMEM((2 * ATTN_BLOCK, 2 * ATTN_BLOCK), F32)],
        compiler_params=pltpu.CompilerParams(
            dimension_semantics=("arbitrary", "arbitrary"),
            vmem_limit_bytes=VMEM_LIMIT),
        name="dilated_attention",
    )(slabs, slabs, slabs)


def _merge_body(x_ref, sc_ref, m0_ref, m1_ref, m2_ref, yssm_ref, yattn_ref,
                scw_ref, pssm_ref, pattn_ref, psc_ref, wout_ref, nw_ref,
                out_ref, ext_scr, *, tiles_per_seq):
    tm = x_ref.shape[1]
    halo = SUBLANES
    first = (pl.program_id(0) % tiles_per_seq) == 0

    @pl.when(first)
    def _():
        ext_scr[0:halo, :] = jnp.zeros((halo, SC_WIDTH), F32)

    @pl.when(jnp.logical_not(first))
    def _():
        ext_scr[0:halo, :] = ext_scr[tm:tm + halo, :]

    u = sc_ref[0, :, 0:SC_WIDTH].astype(F32)
    b_sc = sc_ref[0, :, SC_WIDTH:2 * SC_WIDTH].astype(F32)
    c_sc = sc_ref[0, :, 2 * SC_WIDTH:3 * SC_WIDTH].astype(F32)
    g_sc = sc_ref[0, :, 3 * SC_WIDTH:4 * SC_WIDTH].astype(F32)
    ext_scr[halo:halo + tm, :] = c_sc * u
    conv = jnp.zeros((tm, SC_WIDTH), F32)
    for k in range(SC_CONV):
        lo = halo - (SC_CONV - 1) + k
        conv = conv + scw_ref[k:k + 1, :] * ext_scr[lo:lo + tm, :]
    y_sc = (b_sc * conv * _half_silu(g_sc)).astype(BF16)

    merged = (
        _half_gate2(m0_ref[0].astype(F32))
        * jnp.dot(yssm_ref[0], pssm_ref[...], preferred_element_type=F32)
        + _half_gate2(m1_ref[0].astype(F32))
        * jnp.dot(yattn_ref[0], pattn_ref[...], preferred_element_type=F32)
        + _half_gate2(m2_ref[0].astype(F32))
        * jnp.dot(y_sc, psc_ref[...], preferred_element_type=F32))
    out = jnp.dot(merged.astype(BF16), wout_ref[...], preferred_element_type=F32)
    ms = jnp.mean(out * out, axis=-1, keepdims=True)
    out_ref[0] = x_ref[0] + out * lax.rsqrt(ms + NORM_EPS) * nw_ref[...]


def _merge(x, proj, y_ssm, y_attn, scw, p_ssm, p_attn, p_sc, w_out, nw, tm=512):
    bsz, seq, _ = x.shape
    tps = seq // tm
    row = lambda c: (lambda i: (i // tps, i % tps, c))
    const = lambda i: (0, 0)
    wa = ATTN_KV_WIDTH
    return pl.pallas_call(
        functools.partial(_merge_body, tiles_per_seq=tps),
        grid=(bsz * tps,),
        in_specs=[
            pl.BlockSpec((1, tm, D_MODEL), row(0)),
            pl.BlockSpec((1, tm, 4 * SC_WIDTH), row(COL_SC // (4 * SC_WIDTH))),
            pl.BlockSpec((1, tm, D_MODEL), row(COL_MERGE // D_MODEL)),
            pl.BlockSpec((1, tm, D_MODEL), row(COL_MERGE // D_MODEL + 1)),
            pl.BlockSpec((1, tm, D_MODEL), row(COL_MERGE // D_MODEL + 2)),
            pl.BlockSpec((1, tm, SSM_INNER), row(0)),
            pl.BlockSpec((1, tm, wa), row(0)),
            pl.BlockSpec((SC_CONV, SC_WIDTH), const),
            pl.BlockSpec((SSM_INNER, D_MODEL), const),
            pl.BlockSpec((wa, D_MODEL), const),
            pl.BlockSpec((SC_WIDTH, D_MODEL), const),
            pl.BlockSpec((D_MODEL, D_MODEL), const),
            pl.BlockSpec((1, D_MODEL), const),
        ],
        out_specs=pl.BlockSpec((1, tm, D_MODEL), row(0)),
        out_shape=jax.ShapeDtypeStruct((bsz, seq, D_MODEL), F32),
        scratch_shapes=[pltpu.VMEM((tm + 2 * SUBLANES, SC_WIDTH), F32)],
        compiler_params=pltpu.CompilerParams(
            dimension_semantics=("arbitrary",),
            vmem_limit_bytes=VMEM_LIMIT),
        name="merge_out",
    )(x, proj, proj, proj, proj, y_ssm, y_attn,
      scw, p_ssm, p_attn, p_sc, w_out, nw)


def _expand_matrix(n_heads, width):
    m = np.zeros((2 * LANES, n_heads * width), np.float32)
    for h in range(n_heads):
        m[h, h * width:(h + 1) * width] = 1.0
        m[LANES + h, h * width:(h + 1) * width] = 1.0
    return jnp.asarray(m, BF16)


def _pad_lanes(v):
    return jnp.pad(v.astype(F32), (0, LANES - v.shape[0]))[None, :]


def _prepare_w_in(w):
    o_z, o_xbc, o_dt, o_q, o_k, o_v, o_gat, o_sc, o_mg, o_end = np.cumsum(
        [0, SSM_INNER, SSM_CONV_DIM, SSM_HEADS, 3 * ATTN_KV_WIDTH, ATTN_KV_WIDTH,
         ATTN_KV_WIDTH, ATTN_KV_WIDTH, 4 * SC_WIDTH, 3 * D_MODEL])
    col_scale = np.ones((QKV_WIDTH,), np.float32)
    col_scale[QCOL_Q1:QCOL_K] = ATTN_HEAD_DIM ** -0.5 * LOG2E
    col_scale[QCOL_GATE:] = 0.5
    w_qkv = (w[:, o_q:o_sc] * col_scale).astype(BF16)
    o_gsc = o_sc + 3 * SC_WIDTH
    w_perm = jnp.concatenate(
        [w[:, o_xbc:o_dt], w[:, o_sc:o_gsc], 0.5 * w[:, o_gsc:o_mg], 0.5 * w[:, o_z:o_xbc],
         0.5 * w[:, o_mg:o_end]], axis=1).astype(BF16)
    w_dt = jnp.pad(w[:, o_dt:o_q], ((0, 0), (0, LANES - SSM_HEADS))).astype(BF16)
    return w_qkv, w_perm, w_dt


def kernel(x, norm_pre, norm_post, w_in, ssm_conv_w, ssm_conv_b, dt_bias, a_log,
           d_skip, ssm_norm, sc_conv_w, p_ssm, p_attn, p_sc, w_out):
    bsz, seq, _ = x.shape
    expand_ssm = _expand_matrix(SSM_HEADS, SSM_HEAD_DIM)
    cumsum_mat = _cumsum_matrix()
    shift = _shift_matrix()
    for i in range(norm_pre.shape[0]):
        w_qkv, w_perm, w_dt = _prepare_w_in(w_in[i])
        nw_pre = norm_pre[i].reshape(1, D_MODEL)
        conv_w = (0.5 * ssm_conv_w[i]).astype(BF16)
        conv_b = (0.5 * ssm_conv_b[i]).reshape(1, SSM_CONV_DIM)
        dskip_x = jnp.repeat(d_skip[i].astype(F32), SSM_HEAD_DIM).reshape(1, SSM_INNER)

        slabs = _qkv_proj(x, nw_pre, w_qkv)
        proj, dt_raw = _in_proj(x, nw_pre, w_perm, w_dt)
        y_ssm = _ssd(proj, dt_raw, conv_w, conv_b, _pad_lanes(dt_bias[i]),
                     _pad_lanes(-jnp.exp(a_log[i].astype(F32))), dskip_x,
                     ssm_norm[i].reshape(1, SSM_INNER), expand_ssm, cumsum_mat, shift)
        y_attn = _attention(slabs, bsz, seq)
        x = _merge(x, proj, y_ssm, y_attn, sc_conv_w[i],
                   (0.5 * p_ssm[i]).astype(BF16), (0.5 * p_attn[i]).astype(BF16),
                   (0.5 * p_sc[i]).astype(BF16), w_out[i].astype(BF16),
                   norm_post[i].reshape(1, D_MODEL))
    return x
```

```python
import functools

import jax
import jax.numpy as jnp
import numpy as np
from jax import lax
from jax.experimental import pallas as pl
from jax.experimental.pallas import tpu as pltpu

F32 = jnp.float32
BF16 = jnp.bfloat16

D_MODEL = 1024
SSM_HEADS = 16
SSM_HEAD_DIM = 64
SSM_INNER = SSM_HEADS * SSM_HEAD_DIM
SSM_GROUPS = 4
SSM_HEADS_PER_GROUP = SSM_HEADS // SSM_GROUPS
SSM_STATE = 128
SSM_CONV = 4
SSM_CHUNK = 128
SSM_CONV_DIM = SSM_INNER + 2 * SSM_GROUPS * SSM_STATE
ATTN_HEAD_DIM = 64
ATTN_SLOTS = 8
ATTN_PATTERNS = ((128, 1), (512, 4), (2048, 16))
ATTN_KV_WIDTH = ATTN_SLOTS * ATTN_HEAD_DIM
ATTN_BLOCK = 128
SC_WIDTH = 512
SC_CONV = 3
SSD_SUB = 8
ATTN_UNROLL = 8
NORM_EPS = 1e-6
LOG2E = float(np.log2(np.e))

LANES = 128
SUBLANES = 8
CONV_HALO = 2 * SUBLANES
VMEM_LIMIT = 48 * 1024 * 1024

COL_XBC = 0
COL_SC = COL_XBC + SSM_CONV_DIM
COL_Z = COL_SC + 4 * SC_WIDTH
COL_MERGE = COL_Z + SSM_INNER
PROJ_WIDTH = COL_MERGE + 3 * D_MODEL

PAIRS = ATTN_KV_WIDTH // LANES
SLAB_Q1_Q4 = 0
SLAB_Q16_GATE = SLAB_Q1_Q4 + PAIRS
SLAB_K_V = SLAB_Q16_GATE + PAIRS
N_SLABS = SLAB_K_V + PAIRS
QKV_WIDTH = 2 * N_SLABS * LANES
QCOL_Q1 = 0
QCOL_Q4 = QCOL_Q1 + ATTN_KV_WIDTH
QCOL_Q16 = QCOL_Q4 + ATTN_KV_WIDTH
QCOL_K = QCOL_Q16 + ATTN_KV_WIDTH
QCOL_V = QCOL_K + ATTN_KV_WIDTH
QCOL_GATE = QCOL_V + ATTN_KV_WIDTH
HIGH_HALF = -65536


def _half_silu(hx):
    return hx * jnp.tanh(hx) + hx


def _half_gate2(hx):
    return jnp.tanh(hx) + 1.0


def _pack_pair(a, b):
    ab = lax.bitcast_convert_type(a.astype(BF16).astype(F32), jnp.int32)
    bb = lax.bitcast_convert_type(b.astype(BF16).astype(F32), jnp.int32)
    return ab | lax.shift_right_logical(bb, 16)


def _unpack_high(w):
    return lax.bitcast_convert_type(w & HIGH_HALF, F32)


def _unpack_low(w):
    return lax.bitcast_convert_type(lax.shift_left(w, 16), F32)


def _split_hi_lo(v):
    hi = v.astype(BF16)
    lo = (v - hi.astype(F32)).astype(BF16)
    return jnp.concatenate([hi, lo], axis=1)


def _normed(x_ref, nw_ref):
    x = x_ref[0]
    ms = jnp.mean(x * x, axis=-1, keepdims=True)
    return (x * lax.rsqrt(ms + NORM_EPS) * nw_ref[...]).astype(BF16)


def _qkvproj_body(x_ref, nw_ref, w_ref, o_ref):
    res = jnp.dot(_normed(x_ref, nw_ref), w_ref[...], preferred_element_type=F32)
    for first, (hi_col, lo_col) in ((SLAB_Q1_Q4, (QCOL_Q1, QCOL_Q4)),
                                    (SLAB_Q16_GATE, (QCOL_Q16, QCOL_GATE)),
                                    (SLAB_K_V, (QCOL_K, QCOL_V))):
        for p in range(PAIRS):
            hi = res[:, hi_col + p * LANES:hi_col + (p + 1) * LANES]
            lo = res[:, lo_col + p * LANES:lo_col + (p + 1) * LANES]
            o_ref[first + p] = _pack_pair(hi, lo)


def _qkv_proj(x, norm_w, w_qkv, tm=512):
    bsz, seq, _ = x.shape
    tps = seq // tm
    return pl.pallas_call(
        _qkvproj_body,
        grid=(bsz * tps,),
        in_specs=[
            pl.BlockSpec((1, tm, D_MODEL), lambda i: (i // tps, i % tps, 0)),
            pl.BlockSpec((1, D_MODEL), lambda i: (0, 0)),
            pl.BlockSpec((D_MODEL, QKV_WIDTH), lambda i: (0, 0)),
        ],
        out_specs=pl.BlockSpec((N_SLABS, tm, LANES), lambda i: (0, i, 0)),
        out_shape=jax.ShapeDtypeStruct((N_SLABS, bsz * seq, LANES), jnp.int32),
        compiler_params=pltpu.CompilerParams(
            dimension_semantics=("arbitrary",),
            vmem_limit_bytes=VMEM_LIMIT),
        name="qkv_proj",
    )(x, norm_w, w_qkv)


def _inproj_body(x_ref, nw_ref, w_ref, wdt_ref, o_ref, dt_ref, h_scr):
    @pl.when(pl.program_id(1) == 0)
    def _():
        h = _normed(x_ref, nw_ref)
        h_scr[...] = h
        dt_ref[0] = jnp.dot(h, wdt_ref[...], preferred_element_type=F32)

    o_ref[0] = jnp.dot(h_scr[...], w_ref[...], preferred_element_type=F32).astype(BF16)


def _in_proj(x, norm_w, w_perm, w_dt, tm=1024, tn=2048):
    bsz, seq, _ = x.shape
    tps = seq // tm
    return pl.pallas_call(
        _inproj_body,
        grid=(bsz * tps, PROJ_WIDTH // tn),
        in_specs=[
            pl.BlockSpec((1, tm, D_MODEL), lambda i, j: (i // tps, i % tps, 0)),
            pl.BlockSpec((1, D_MODEL), lambda i, j: (0, 0)),
            pl.BlockSpec((D_MODEL, tn), lambda i, j: (0, j)),
            pl.BlockSpec((D_MODEL, LANES), lambda i, j: (0, 0)),
        ],
        out_specs=[
            pl.BlockSpec((1, tm, tn), lambda i, j: (i // tps, i % tps, j)),
            pl.BlockSpec((1, tm, LANES), lambda i, j: (i // tps, i % tps, 0)),
        ],
        out_shape=[
            jax.ShapeDtypeStruct((bsz, seq, PROJ_WIDTH), BF16),
            jax.ShapeDtypeStruct((bsz, seq, LANES), F32),
        ],
        scratch_shapes=[pltpu.VMEM((tm, D_MODEL), BF16)],
        compiler_params=pltpu.CompilerParams(
            dimension_semantics=("arbitrary", "arbitrary"),
            vmem_limit_bytes=VMEM_LIMIT),
        name="in_proj",
    )(x, norm_w, w_perm, w_dt)


def _ssd_body(xbc_ref, z_ref, dt_ref, cw_ref, cb_ref, dtb_ref, aneg_ref,
              dskip_ref, nw_ref, expand_ref, tri_ref, shift_ref, o_ref,
              prev_scr, xs_scr, b_scr, c_scr, bd_scr, state_scr):
    q = SSM_CHUNK

    @pl.when(pl.program_id(1) == 0)
    def _():
        prev_scr[...] = jnp.zeros_like(prev_scr)
        state_scr[...] = jnp.zeros_like(state_scr)
        bd_scr[...] = jnp.zeros_like(bd_scr)

    row = lax.broadcasted_iota(jnp.int32, (q, q), 0)
    col = lax.broadcasted_iota(jnp.int32, (q, q), 1)
    causal = row >= col

    for sub in range(SSD_SUB):
        rows = slice(sub * q, (sub + 1) * q)
        _ssd_chunk(
            xbc_ref.at[0, rows, :],
            prev_scr if sub == 0 else xbc_ref.at[0, slice((sub - 1) * q, sub * q), :],
            z_ref.at[0, rows, :], dt_ref.at[0, rows, :], cw_ref, cb_ref, dtb_ref, aneg_ref,
            dskip_ref, nw_ref, expand_ref, tri_ref, shift_ref, o_ref.at[0, rows, :],
            xs_scr.at[rows, :], b_scr.at[rows, :], c_scr.at[rows, :], bd_scr.at[sub],
            state_scr, causal)
    prev_scr[...] = xbc_ref[0, (SSD_SUB - 1) * q:SSD_SUB * q, :]


def _ssd_chunk(xbc_ref, prev_ref, z_ref, dt_ref, cw_ref, cb_ref, dtb_ref, aneg_ref,
               dskip_ref, nw_ref, expand_ref, tri_ref, shift_ref, o_ref,
               xs_scr, b_scr, c_scr, bd_scr, state_scr, causal):
    q = SSM_CHUNK

    piece = 512
    for p in range(SSM_CONV_DIM // piece):
        cols = slice(p * piece, (p + 1) * piece)
        win = jnp.concatenate([prev_ref[q - CONV_HALO:q, cols], xbc_ref[:, cols]], axis=0)
        taps = jnp.concatenate([win * cw_ref[k:k + 1, cols] for k in range(SSM_CONV)], axis=0)
        acc = jnp.dot(shift_ref[...], taps, preferred_element_type=F32) + cb_ref[:, cols]
        act = _half_silu(acc)
        if p < 2:
            xs_scr[:, cols] = act
        elif p == 2:
            b_scr[...] = act.astype(BF16)
        else:
            c_scr[...] = act.astype(BF16)

    dt_in = dt_ref[...] + dtb_ref[...]
    dt = jnp.maximum(dt_in, 0.0) + jnp.log(1.0 + jnp.exp(-jnp.abs(dt_in)))
    a = dt * aneg_ref[...]
    a_hi = a.astype(BF16)
    r1 = a - a_hi.astype(F32)
    a_mid = r1.astype(BF16)
    a_lo = (r1 - a_mid.astype(F32)).astype(BF16)
    cs = jnp.dot(tri_ref[...], jnp.concatenate([a_hi, a_mid, a_lo], axis=0),
                 preferred_element_type=F32)
    cs2 = cs * LOG2E
    cs2_t = cs2.T
    ecs = jnp.exp(cs)
    dec = jnp.exp(cs[q - 1:q, :] - cs)
    stacked = jnp.concatenate(
        [_split_hi_lo(dt), _split_hi_lo(ecs), _split_hi_lo(dec)], axis=0)
    expanded = jnp.dot(stacked, expand_ref[...], preferred_element_type=F32)
    dt_x = expanded[0:q]
    ecs_x = expanded[q:2 * q]
    dec_x = expanded[2 * q:3 * q]

    xs = xs_scr[...]
    xdt = xs * dt_x
    xdt_b = xdt.astype(BF16)
    xdec_b = (xdt * dec_x).astype(BF16)

    gw = SSM_HEADS_PER_GROUP * SSM_HEAD_DIM
    for g in range(SSM_GROUPS):
        gcols = slice(g * gw, (g + 1) * gw)
        ncols = slice(g * SSM_STATE, (g + 1) * SSM_STATE)
        bg = b_scr[:, ncols]
        cg = c_scr[:, ncols]
        cb = lax.dot_general(cg, bg, (((1,), (1,)), ((), ())),
                             preferred_element_type=F32)
        st = state_scr[:, gcols]
        y_off = jnp.dot(cg, st.astype(BF16),
                        preferred_element_type=F32) * ecs_x[:, gcols]
        m_parts = []
        for j in range(SSM_HEADS_PER_GROUP):
            h = g * SSM_HEADS_PER_GROUP + j
            seg = cs2[:, h:h + 1] - cs2_t[h:h + 1, :]
            lmat = jnp.exp2(jnp.where(causal, seg, -jnp.inf))
            m_parts.append((cb * lmat).astype(BF16))
            bd_scr[g, j * q:(j + 1) * q, j * SSM_HEAD_DIM:(j + 1) * SSM_HEAD_DIM] = (
                xdt_b[:, h * SSM_HEAD_DIM:(h + 1) * SSM_HEAD_DIM])
        y_g = jnp.dot(jnp.concatenate(m_parts, axis=1), bd_scr[g],
                      preferred_element_type=F32) + y_off
        upd = lax.dot_general(bg, xdec_b[:, gcols], (((0,), (0,)), ((), ())),
                              preferred_element_type=F32)
        state_scr[:, gcols] = st * ecs_x[q - 1:q, gcols] + upd

        y_g = y_g + xs[:, gcols] * dskip_ref[:, gcols]
        yg = y_g * _half_silu(z_ref[:, gcols].astype(F32))
        ms = jnp.mean(yg * yg, axis=-1, keepdims=True)
        yg = yg * lax.rsqrt(ms + NORM_EPS) * nw_ref[:, gcols]
        o_ref[:, gcols] = yg.astype(BF16)


def _shift_matrix():
    q = SSM_CHUNK
    win = q + CONV_HALO
    m = np.zeros((q, SSM_CONV * win), np.float32)
    for k in range(SSM_CONV):
        for t in range(q):
            m[t, k * win + CONV_HALO + t - (SSM_CONV - 1) + k] = 1.0
    return jnp.asarray(m, BF16)


def _cumsum_matrix():
    tri = np.tril(np.ones((SSM_CHUNK, SSM_CHUNK), np.float32))
    return jnp.asarray(np.concatenate([tri, tri, tri], axis=1), BF16)


def _ssd(proj3, dt3, cw, cb, dtb, aneg, dskip_x, nw, expand_mat, tri, shift):
    bsz, s, _ = proj3.shape
    q = SSM_CHUNK
    rows = SSD_SUB * q
    const = lambda b, c: (0, 0)
    return pl.pallas_call(
        _ssd_body,
        grid=(bsz, s // rows),
        in_specs=[
            pl.BlockSpec((1, rows, SSM_CONV_DIM), lambda b, c: (b, c, COL_XBC // SSM_CONV_DIM)),
            pl.BlockSpec((1, rows, SSM_INNER), lambda b, c: (b, c, COL_Z // SSM_INNER)),
            pl.BlockSpec((1, rows, LANES), lambda b, c: (b, c, 0)),
            pl.BlockSpec((SSM_CONV, SSM_CONV_DIM), const),
            pl.BlockSpec((1, SSM_CONV_DIM), const),
            pl.BlockSpec((1, LANES), const),
            pl.BlockSpec((1, LANES), const),
            pl.BlockSpec((1, SSM_INNER), const),
            pl.BlockSpec((1, SSM_INNER), const),
            pl.BlockSpec((2 * LANES, SSM_INNER), const),
            pl.BlockSpec((q, 3 * q), const),
            pl.BlockSpec((q, SSM_CONV * (q + CONV_HALO)), const),
        ],
        out_specs=pl.BlockSpec((1, rows, SSM_INNER), lambda b, c: (b, c, 0)),
        out_shape=jax.ShapeDtypeStruct((bsz, s, SSM_INNER), BF16),
        scratch_shapes=[
            pltpu.VMEM((q, SSM_CONV_DIM), BF16),
            pltpu.VMEM((rows, SSM_INNER), F32),
            pltpu.VMEM((rows, SSM_GROUPS * SSM_STATE), BF16),
            pltpu.VMEM((rows, SSM_GROUPS * SSM_STATE), BF16),
            pltpu.VMEM((SSD_SUB, SSM_GROUPS, SSM_HEADS_PER_GROUP * q,
                        SSM_HEADS_PER_GROUP * SSM_HEAD_DIM), BF16),
            pltpu.VMEM((SSM_STATE, SSM_INNER), F32),
        ],
        compiler_params=pltpu.CompilerParams(
            dimension_semantics=("arbitrary", "arbitrary"),
            vmem_limit_bytes=VMEM_LIMIT),
        name="ssd",
    )(proj3, proj3, dt3, cw, cb, dtb, aneg, dskip_x, nw, expand_mat, tri, shift)


def _attn_block(q, k, v, bias):
    nq, nk = bias.shape
    low = lax.broadcasted_iota(jnp.int32, (nq, LANES), 1) < ATTN_HEAD_DIM
    q2 = jnp.concatenate([jnp.where(low, q, 0.0), jnp.where(low, 0.0, q)],
                         axis=0).astype(BF16)
    s = lax.dot_general(q2, k.astype(BF16), (((1,), (1,)), ((), ())),
                        preferred_element_type=F32)
    s = s + jnp.concatenate([bias, bias], axis=0)
    mx = jnp.max(s, axis=-1, keepdims=True)
    p = jnp.exp2(s - mx).astype(BF16)
    vext = jnp.concatenate([v.astype(BF16), jnp.ones((nk, LANES), BF16)], axis=1)
    r = jnp.dot(p, vext, preferred_element_type=F32)
    acc = jnp.where(low, r[0:nq, 0:LANES], r[nq:, 0:LANES])
    den = jnp.where(low, r[0:nq, LANES:], r[nq:, LANES:])
    mxp = jnp.where(low, mx[0:nq], mx[nq:])
    return acc / den, mxp + jnp.log2(den)


def _attn_body(qa_ref, qb_ref, kv_ref, out_ref,
               o4_scr, l4_scr, o16_scr, l16_scr, bias_scr, wbias_scr):
    blk = ATTN_BLOCK
    seq = kv_ref.shape[1]
    (w1, d1), (w4, d4), (w16, d16) = ATTN_PATTERNS
    n_back = w1 // d1
    assert n_back == w4 // d4 == w16 // d16 == blk

    qi = lax.broadcasted_iota(jnp.int32, (blk, 2 * blk), 0) + blk
    kj = lax.broadcasted_iota(jnp.int32, (blk, 2 * blk), 1)
    diff = qi - kj
    band = (diff >= 0) & (diff <= n_back)
    bias_scr[0] = jnp.where(band & (kj >= blk), 0.0, -jnp.inf)
    bias_scr[1] = jnp.where(band, 0.0, -jnp.inf)
    wdiff = (lax.broadcasted_iota(jnp.int32, (2 * blk, 2 * blk), 0)
             - lax.broadcasted_iota(jnp.int32, (2 * blk, 2 * blk), 1))
    wbias_scr[...] = jnp.where((wdiff >= 0) & (wdiff <= n_back), 0.0, -jnp.inf)

    def whole_phase(q_ref, unpack_q, o_scr, l_scr, dil):
        length = seq // dil
        assert length == 2 * blk

        def body(r, carry):
            rows = pl.ds(r, length, stride=dil)
            kv = kv_ref[0, rows, :]
            o, lse = _attn_block(unpack_q(q_ref[0, rows, :]), _unpack_high(kv), _unpack_low(kv),
                                 wbias_scr[...])
            o_scr[rows, :] = o
            l_scr[rows, :] = lse
            return carry

        lax.fori_loop(0, dil, body, 0, unroll=ATTN_UNROLL // 2)

    def strided_phase(q_ref, unpack_q, o_scr, l_scr, dil):
        span = blk * dil

        def body(idx, carry):
            r = idx % dil
            n = idx // dil
            cur = r + span * n
            prv = jnp.maximum(cur - span, r)

            def ld(ref, start):
                return ref[0, pl.ds(start, blk, stride=dil), :]

            kv = jnp.concatenate([ld(kv_ref, prv), ld(kv_ref, cur)], axis=0)
            o, lse = _attn_block(unpack_q(ld(q_ref, cur)), _unpack_high(kv), _unpack_low(kv),
                                 bias_scr[jnp.minimum(n, 1)])
            o_scr[pl.ds(cur, blk, stride=dil), :] = o
            l_scr[pl.ds(cur, blk, stride=dil), :] = lse
            return carry

        lax.fori_loop(0, seq // blk, body, 0, unroll=ATTN_UNROLL)

    whole_phase(qb_ref, _unpack_high, o16_scr, l16_scr, d16)
    strided_phase(qa_ref, _unpack_low, o4_scr, l4_scr, d4)

    def body(n, carry):
        cur = pl.multiple_of(n * blk, blk)
        prv = pl.multiple_of(jnp.maximum(cur - blk, 0), blk)
        rows = pl.ds(cur, blk)
        prows = pl.ds(prv, blk)
        kv = jnp.concatenate([kv_ref[0, prows, :], kv_ref[0, rows, :]], axis=0)
        o1, l1 = _attn_block(_unpack_high(qa_ref[0, rows, :]), _unpack_high(kv), _unpack_low(kv),
                             bias_scr[jnp.minimum(n, 1)])
        l4, l16 = l4_scr[rows, :], l16_scr[rows, :]
        lm = jnp.maximum(jnp.maximum(l1, l4), l16)
        e1, e4, e16 = jnp.exp2(l1 - lm), jnp.exp2(l4 - lm), jnp.exp2(l16 - lm)
        y = (e1 * o1 + e4 * o4_scr[rows, :] + e16 * o16_scr[rows, :]) / (e1 + e4 + e16)
        gate = _half_silu(_unpack_low(qb_ref[0, rows, :]))
        out_ref[0, rows, :] = (y * gate).astype(BF16)
        return carry

    lax.fori_loop(0, seq // blk, body, 0, unroll=ATTN_UNROLL)


def _attention(slabs, bsz, seq):
    slab = lambda s0: (lambda b, p: (s0 + p, b, 0))
    spec = lambda s0: pl.BlockSpec((1, seq, LANES), slab(s0))
    return pl.pallas_call(
        _attn_body,
        grid=(bsz, PAIRS),
        in_specs=[spec(SLAB_Q1_Q4), spec(SLAB_Q16_GATE), spec(SLAB_K_V)],
        out_specs=pl.BlockSpec((1, seq, LANES), lambda b, p: (b, 0, p)),
        out_shape=jax.ShapeDtypeStruct((bsz, seq, ATTN_KV_WIDTH), BF16),
        scratch_shapes=[pltpu.VMEM((seq, LANES), F32) for _ in range(4)]
        + [pltpu.VMEM((2, ATTN_BLOCK, 2 * ATTN_BLOCK), F32),
           pltpu.VMEM((2 * ATTN_BLOCK, 2 * ATTN_BLOCK), F32)],
        compiler_params=pltpu.CompilerParams(
            dimension_semantics=("arbitrary", "arbitrary"),
            vmem_limit_bytes=VMEM_LIMIT),
        name="dilated_attention",
    )(slabs, slabs, slabs)


def _merge_body(x_ref, sc_ref, m0_ref, m1_ref, m2_ref, yssm_ref, yattn_ref,
                scw_ref, pssm_ref, pattn_ref, psc_ref, wout_ref, nw_ref,
                out_ref, ext_scr, *, tiles_per_seq):
    tm = x_ref.shape[1]
    halo = SUBLANES
    first = (pl.program_id(0) % tiles_per_seq) == 0

    @pl.when(first)
    def _():
        ext_scr[0:halo, :] = jnp.zeros((halo, SC_WIDTH), F32)

    @pl.when(jnp.logical_not(first))
    def _():
        ext_scr[0:halo, :] = ext_scr[tm:tm + halo, :]

    u = sc_ref[0, :, 0:SC_WIDTH].astype(F32)
    b_sc = sc_ref[0, :, SC_WIDTH:2 * SC_WIDTH].astype(F32)
    c_sc = sc_ref[0, :, 2 * SC_WIDTH:3 * SC_WIDTH].astype(F32)
    g_sc = sc_ref[0, :, 3 * SC_WIDTH:4 * SC_WIDTH].astype(F32)
    ext_scr[halo:halo + tm, :] = c_sc * u
    conv = jnp.zeros((tm, SC_WIDTH), F32)
    for k in range(SC_CONV):
        lo = halo - (SC_CONV - 1) + k
        conv = conv + scw_ref[k:k + 1, :] * ext_scr[lo:lo + tm, :]
    y_sc = (b_sc * conv * _half_silu(g_sc)).astype(BF16)

    merged = (
        _half_gate2(m0_ref[0].astype(F32))
        * jnp.dot(yssm_ref[0], pssm_ref[...], preferred_element_type=F32)
        + _half_gate2(m1_ref[0].astype(F32))
        * jnp.dot(yattn_ref[0], pattn_ref[...], preferred_element_type=F32)
        + _half_gate2(m2_ref[0].astype(F32))
        * jnp.dot(y_sc, psc_ref[...], preferred_element_type=F32))
    out = jnp.dot(merged.astype(BF16), wout_ref[...], preferred_element_type=F32)
    ms = jnp.mean(out * out, axis=-1, keepdims=True)
    out_ref[0] = x_ref[0] + out * lax.rsqrt(ms + NORM_EPS) * nw_ref[...]


def _merge(x, proj, y_ssm, y_attn, scw, w_stack, nw, tm=512):
    bsz, seq, _ = x.shape
    tps = seq // tm
    row = lambda c: (lambda i: (i // tps, i % tps, c))
    const = lambda i: (0, 0)
    wrows = lambda r: (lambda i: (r, 0))
    wa = ATTN_KV_WIDTH
    assert SSM_INNER == D_MODEL == 2 * wa and wa == SC_WIDTH
    return pl.pallas_call(
        functools.partial(_merge_body, tiles_per_seq=tps),
        grid=(bsz * tps,),
        in_specs=[
            pl.BlockSpec((1, tm, D_MODEL), row(0)),
            pl.BlockSpec((1, tm, 4 * SC_WIDTH), row(COL_SC // (4 * SC_WIDTH))),
            pl.BlockSpec((1, tm, D_MODEL), row(COL_MERGE // D_MODEL)),
            pl.BlockSpec((1, tm, D_MODEL), row(COL_MERGE // D_MODEL + 1)),
            pl.BlockSpec((1, tm, D_MODEL), row(COL_MERGE // D_MODEL + 2)),
            pl.BlockSpec((1, tm, SSM_INNER), row(0)),
            pl.BlockSpec((1, tm, wa), row(0)),
            pl.BlockSpec((SC_CONV, SC_WIDTH), const),
            pl.BlockSpec((SSM_INNER, D_MODEL), wrows(0)),
            pl.BlockSpec((wa, D_MODEL), wrows(2)),
            pl.BlockSpec((SC_WIDTH, D_MODEL), wrows(3)),
            pl.BlockSpec((D_MODEL, D_MODEL), wrows(2)),
            pl.BlockSpec((1, D_MODEL), const),
        ],
        out_specs=pl.BlockSpec((1, tm, D_MODEL), row(0)),
        out_shape=jax.ShapeDtypeStruct((bsz, seq, D_MODEL), F32),
        scratch_shapes=[pltpu.VMEM((tm + 2 * SUBLANES, SC_WIDTH), F32)],
        compiler_params=pltpu.CompilerParams(
            dimension_semantics=("arbitrary",),
            vmem_limit_bytes=VMEM_LIMIT),
        name="merge_out",
    )(x, proj, proj, proj, proj, y_ssm, y_attn,
      scw, w_stack, w_stack, w_stack, w_stack, nw)


def _expand_matrix(n_heads, width):
    m = np.zeros((2 * LANES, n_heads * width), np.float32)
    for h in range(n_heads):
        m[h, h * width:(h + 1) * width] = 1.0
        m[LANES + h, h * width:(h + 1) * width] = 1.0
    return jnp.asarray(m, BF16)


def _pad_lanes(v):
    return jnp.pad(v.astype(F32), (0, LANES - v.shape[0]))[None, :]


def _prepare_w_in(w):
    o_z, o_xbc, o_dt, o_q, o_k, o_v, o_gat, o_sc, o_mg, o_end = np.cumsum(
        [0, SSM_INNER, SSM_CONV_DIM, SSM_HEADS, 3 * ATTN_KV_WIDTH, ATTN_KV_WIDTH,
         ATTN_KV_WIDTH, ATTN_KV_WIDTH, 4 * SC_WIDTH, 3 * D_MODEL])
    col_scale = np.ones((QKV_WIDTH,), np.float32)
    col_scale[QCOL_Q1:QCOL_K] = ATTN_HEAD_DIM ** -0.5 * LOG2E
    col_scale[QCOL_GATE:] = 0.5
    w_qkv = (w[:, o_q:o_sc] * col_scale).astype(BF16)
    o_gsc = o_sc + 3 * SC_WIDTH
    w_perm = jnp.concatenate(
        [w[:, o_xbc:o_dt], w[:, o_sc:o_gsc], 0.5 * w[:, o_gsc:o_mg], 0.5 * w[:, o_z:o_xbc],
         0.5 * w[:, o_mg:o_end]], axis=1).astype(BF16)
    w_dt = jnp.pad(w[:, o_dt:o_q], ((0, 0), (0, LANES - SSM_HEADS))).astype(BF16)
    return w_qkv, w_perm, w_dt


def kernel(x, norm_pre, norm_post, w_in, ssm_conv_w, ssm_conv_b, dt_bias, a_log,
           d_skip, ssm_norm, sc_conv_w, p_ssm, p_attn, p_sc, w_out):
    bsz, seq, _ = x.shape
    expand_ssm = _expand_matrix(SSM_HEADS, SSM_HEAD_DIM)
    cumsum_mat = _cumsum_matrix()
    shift = _shift_matrix()
    for i in range(norm_pre.shape[0]):
        w_qkv, w_perm, w_dt = _prepare_w_in(w_in[i])
        nw_pre = norm_pre[i].reshape(1, D_MODEL)
        conv_w = (0.5 * ssm_conv_w[i]).astype(BF16)
        conv_b = (0.5 * ssm_conv_b[i]).reshape(1, SSM_CONV_DIM)
        dskip_x = jnp.repeat(d_skip[i].astype(F32), SSM_HEAD_DIM).reshape(1, SSM_INNER)

        slabs = _qkv_proj(x, nw_pre, w_qkv)
        proj, dt_raw = _in_proj(x, nw_pre, w_perm, w_dt)
        y_ssm = _ssd(proj, dt_raw, conv_w, conv_b, _pad_lanes(dt_bias[i]),
                     _pad_lanes(-jnp.exp(a_log[i].astype(F32))), dskip_x,
                     ssm_norm[i].reshape(1, SSM_INNER), expand_ssm, cumsum_mat, shift)
        y_attn = _attention(slabs, bsz, seq)
        w_stack = jnp.concatenate(
            [0.5 * p_ssm[i], 0.5 * p_attn[i], 0.5 * p_sc[i], w_out[i]], axis=0).astype(BF16)
        x = _merge(x, proj, y_ssm, y_attn, sc_conv_w[i], w_stack,
                   norm_post[i].reshape(1, D_MODEL))
    return x
```

```python
import functools

import jax
import jax.numpy as jnp
import numpy as np
from jax import lax
from jax.experimental import pallas as pl
from jax.experimental.pallas import tpu as pltpu

F32 = jnp.float32
BF16 = jnp.bfloat16

D_MODEL = 1024
SSM_HEADS = 16
SSM_HEAD_DIM = 64
SSM_INNER = SSM_HEADS * SSM_HEAD_DIM
SSM_GROUPS = 4
SSM_HEADS_PER_GROUP = SSM_HEADS // SSM_GROUPS
SSM_STATE = 128
SSM_CONV = 4
SSM_CHUNK = 128
SSM_CONV_DIM = SSM_INNER + 2 * SSM_GROUPS * SSM_STATE
ATTN_HEAD_DIM = 64
ATTN_SLOTS = 8
ATTN_PATTERNS = ((128, 1), (512, 4), (2048, 16))
ATTN_KV_WIDTH = ATTN_SLOTS * ATTN_HEAD_DIM
ATTN_BLOCK = 128
SC_WIDTH = 512
SC_CONV = 3
SSD_SUB = 8
ATTN_UNROLL = 8
NORM_EPS = 1e-6
LOG2E = float(np.log2(np.e))

LANES = 128
SUBLANES = 8
CONV_HALO = 2 * SUBLANES
VMEM_LIMIT = 48 * 1024 * 1024

COL_XBC = 0
COL_SC = COL_XBC + SSM_CONV_DIM
COL_Z = COL_SC + 4 * SC_WIDTH
COL_MERGE = COL_Z + SSM_INNER
PROJ_WIDTH = COL_MERGE + 3 * D_MODEL

PAIRS = ATTN_KV_WIDTH // LANES
SLAB_Q1_Q4 = 0
SLAB_Q16_GATE = SLAB_Q1_Q4 + PAIRS
SLAB_K_V = SLAB_Q16_GATE + PAIRS
N_SLABS = SLAB_K_V + PAIRS
QKV_WIDTH = 2 * N_SLABS * LANES
QCOL_Q1 = 0
QCOL_Q4 = QCOL_Q1 + ATTN_KV_WIDTH
QCOL_Q16 = QCOL_Q4 + ATTN_KV_WIDTH
QCOL_K = QCOL_Q16 + ATTN_KV_WIDTH
QCOL_V = QCOL_K + ATTN_KV_WIDTH
QCOL_GATE = QCOL_V + ATTN_KV_WIDTH
HIGH_HALF = -65536


def _half_silu(hx):
    return hx * jnp.tanh(hx) + hx


def _half_gate2(hx):
    return jnp.tanh(hx) + 1.0


def _pack_pair(a, b):
    ab = lax.bitcast_convert_type(a.astype(BF16).astype(F32), jnp.int32)
    bb = lax.bitcast_convert_type(b.astype(BF16).astype(F32), jnp.int32)
    return ab | lax.shift_right_logical(bb, 16)


def _unpack_high(w):
    return lax.bitcast_convert_type(w & HIGH_HALF, F32)


def _unpack_low(w):
    return lax.bitcast_convert_type(lax.shift_left(w, 16), F32)


def _split_hi_lo(v):
    hi = v.astype(BF16)
    lo = (v - hi.astype(F32)).astype(BF16)
    return jnp.concatenate([hi, lo], axis=1)


def _normed(x_ref, nw_ref):
    x = x_ref[0]
    ms = jnp.mean(x * x, axis=-1, keepdims=True)
    return (x * lax.rsqrt(ms + NORM_EPS) * nw_ref[...]).astype(BF16)


PREP_TILE = 1024


def _wprep_body(base_ref, shift_ref, main_ref, next_ref, scale_ref, o_ref):
    main = main_ref[...]
    both = jnp.concatenate([main, next_ref[...]], axis=1)
    moved = both[:, SSM_HEADS:SSM_HEADS + PREP_TILE]
    src = jnp.where(shift_ref[pl.program_id(1)] > 0, moved, main)
    o_ref[...] = (src * scale_ref[...]).astype(BF16)


def _prep_weights(w_in, src_cols, col_scale, name):
    depth = w_in.shape[0]
    n_tiles = len(src_cols)
    assert all(c % PREP_TILE in (0, SSM_HEADS) for c in src_cols)
    base = jnp.asarray([c // PREP_TILE for c in src_cols], jnp.int32)
    shift = jnp.asarray([c % PREP_TILE for c in src_cols], jnp.int32)
    per_tile = PREP_TILE // LANES
    grid_spec = pltpu.PrefetchScalarGridSpec(
        num_scalar_prefetch=2,
        grid=(depth, n_tiles),
        in_specs=[
            pl.BlockSpec((None, D_MODEL, PREP_TILE), lambda l, t, b, s: (l, 0, b[t])),
            pl.BlockSpec((None, D_MODEL, LANES), lambda l, t, b, s: (l, 0, (b[t] + 1) * per_tile)),
            pl.BlockSpec((1, PREP_TILE), lambda l, t, b, s: (0, t)),
        ],
        out_specs=pl.BlockSpec((None, D_MODEL, PREP_TILE), lambda l, t, b, s: (l, 0, t)),
    )
    return pl.pallas_call(
        _wprep_body,
        grid_spec=grid_spec,
        out_shape=jax.ShapeDtypeStruct((depth, D_MODEL, n_tiles * PREP_TILE), BF16),
        compiler_params=pltpu.CompilerParams(
            dimension_semantics=("arbitrary", "arbitrary"),
            vmem_limit_bytes=VMEM_LIMIT),
        name=name,
    )(base, shift, w_in, w_in, jnp.asarray(col_scale, F32).reshape(1, -1))


def _qkvproj_body(x_ref, nw_ref, w_ref, o_ref):
    res = jnp.dot(_normed(x_ref, nw_ref), w_ref[...], preferred_element_type=F32)
    for first, (hi_col, lo_col) in ((SLAB_Q1_Q4, (QCOL_Q1, QCOL_Q4)),
                                    (SLAB_Q16_GATE, (QCOL_Q16, QCOL_GATE)),
                                    (SLAB_K_V, (QCOL_K, QCOL_V))):
        for p in range(PAIRS):
            hi = res[:, hi_col + p * LANES:hi_col + (p + 1) * LANES]
            lo = res[:, lo_col + p * LANES:lo_col + (p + 1) * LANES]
            o_ref[first + p] = _pack_pair(hi, lo)


def _qkv_proj(x, norm_w, w_qkv, layer, tm=512):
    bsz, seq, _ = x.shape
    tps = seq // tm
    return pl.pallas_call(
        _qkvproj_body,
        grid=(bsz * tps,),
        in_specs=[
            pl.BlockSpec((1, tm, D_MODEL), lambda i: (i // tps, i % tps, 0)),
            pl.BlockSpec((1, D_MODEL), lambda i: (0, 0)),
            pl.BlockSpec((None, D_MODEL, QKV_WIDTH), lambda i: (layer, 0, 0)),
        ],
        out_specs=pl.BlockSpec((N_SLABS, tm, LANES), lambda i: (0, i, 0)),
        out_shape=jax.ShapeDtypeStruct((N_SLABS, bsz * seq, LANES), jnp.int32),
        compiler_params=pltpu.CompilerParams(
            dimension_semantics=("arbitrary",),
            vmem_limit_bytes=VMEM_LIMIT),
        name="qkv_proj",
    )(x, norm_w, w_qkv)


def _inproj_body(x_ref, nw_ref, w_ref, wdt_ref, o_ref, dt_ref, h_scr):
    @pl.when(pl.program_id(1) == 0)
    def _():
        h = _normed(x_ref, nw_ref)
        h_scr[...] = h
        dt_ref[0] = jnp.dot(h, wdt_ref[...], preferred_element_type=F32)

    o_ref[0] = jnp.dot(h_scr[...], w_ref[...], preferred_element_type=F32).astype(BF16)


def _in_proj(x, norm_w, w_perm, layer, w_dt, tm=1024, tn=2048):
    bsz, seq, _ = x.shape
    tps = seq // tm
    return pl.pallas_call(
        _inproj_body,
        grid=(bsz * tps, PROJ_WIDTH // tn),
        in_specs=[
            pl.BlockSpec((1, tm, D_MODEL), lambda i, j: (i // tps, i % tps, 0)),
            pl.BlockSpec((1, D_MODEL), lambda i, j: (0, 0)),
            pl.BlockSpec((None, D_MODEL, tn), lambda i, j: (layer, 0, j)),
            pl.BlockSpec((D_MODEL, LANES), lambda i, j: (0, 0)),
        ],
        out_specs=[
            pl.BlockSpec((1, tm, tn), lambda i, j: (i // tps, i % tps, j)),
            pl.BlockSpec((1, tm, LANES), lambda i, j: (i // tps, i % tps, 0)),
        ],
        out_shape=[
            jax.ShapeDtypeStruct((bsz, seq, PROJ_WIDTH), BF16),
            jax.ShapeDtypeStruct((bsz, seq, LANES), F32),
        ],
        scratch_shapes=[pltpu.VMEM((tm, D_MODEL), BF16)],
        compiler_params=pltpu.CompilerParams(
            dimension_semantics=("arbitrary", "arbitrary"),
            vmem_limit_bytes=VMEM_LIMIT),
        name="in_proj",
    )(x, norm_w, w_perm, w_dt)


def _ssd_body(xbc_ref, z_ref, dt_ref, cw_ref, cb_ref, dtb_ref, aneg_ref,
              dskip_ref, nw_ref, expand_ref, tri_ref, shift_ref, o_ref,
              prev_scr, xs_scr, b_scr, c_scr, bd_scr, state_scr):
    q = SSM_CHUNK

    @pl.when(pl.program_id(1) == 0)
    def _():
        prev_scr[...] = jnp.zeros_like(prev_scr)
        state_scr[...] = jnp.zeros_like(state_scr)
        bd_scr[...] = jnp.zeros_like(bd_scr)

    row = lax.broadcasted_iota(jnp.int32, (q, q), 0)
    col = lax.broadcasted_iota(jnp.int32, (q, q), 1)
    causal = row >= col

    for sub in range(SSD_SUB):
        rows = slice(sub * q, (sub + 1) * q)
        _ssd_chunk(
            xbc_ref.at[0, rows, :],
            prev_scr if sub == 0 else xbc_ref.at[0, slice((sub - 1) * q, sub * q), :],
            z_ref.at[0, rows, :], dt_ref.at[0, rows, :], cw_ref, cb_ref, dtb_ref, aneg_ref,
            dskip_ref, nw_ref, expand_ref, tri_ref, shift_ref, o_ref.at[0, rows, :],
            xs_scr.at[rows, :], b_scr.at[rows, :], c_scr.at[rows, :], bd_scr.at[sub],
            state_scr, causal)
    prev_scr[...] = xbc_ref[0, (SSD_SUB - 1) * q:SSD_SUB * q, :]


def _ssd_chunk(xbc_ref, prev_ref, z_ref, dt_ref, cw_ref, cb_ref, dtb_ref, aneg_ref,
               dskip_ref, nw_ref, expand_ref, tri_ref, shift_ref, o_ref,
               xs_scr, b_scr, c_scr, bd_scr, state_scr, causal):
    q = SSM_CHUNK

    piece = 512
    for p in range(SSM_CONV_DIM // piece):
        cols = slice(p * piece, (p + 1) * piece)
        win = jnp.concatenate([prev_ref[q - CONV_HALO:q, cols], xbc_ref[:, cols]], axis=0)
        taps = jnp.concatenate([win * cw_ref[k:k + 1, cols] for k in range(SSM_CONV)], axis=0)
        acc = jnp.dot(shift_ref[...], taps, preferred_element_type=F32) + cb_ref[:, cols]
        act = _half_silu(acc)
        if p < 2:
            xs_scr[:, cols] = act
        elif p == 2:
            b_scr[...] = act.astype(BF16)
        else:
            c_scr[...] = act.astype(BF16)

    dt_in = dt_ref[...] + dtb_ref[...]
    dt = jnp.maximum(dt_in, 0.0) + jnp.log(1.0 + jnp.exp(-jnp.abs(dt_in)))
    a = dt * aneg_ref[...]
    a_hi = a.astype(BF16)
    r1 = a - a_hi.astype(F32)
    a_mid = r1.astype(BF16)
    a_lo = (r1 - a_mid.astype(F32)).astype(BF16)
    cs = jnp.dot(tri_ref[...], jnp.concatenate([a_hi, a_mid, a_lo], axis=0),
                 preferred_element_type=F32)
    cs2 = cs * LOG2E
    cs2_t = cs2.T
    ecs = jnp.exp(cs)
    dec = jnp.exp(cs[q - 1:q, :] - cs)
    stacked = jnp.concatenate(
        [_split_hi_lo(dt), _split_hi_lo(ecs), _split_hi_lo(dec)], axis=0)
    expanded = jnp.dot(stacked, expand_ref[...], preferred_element_type=F32)
    dt_x = expanded[0:q]
    ecs_x = expanded[q:2 * q]
    dec_x = expanded[2 * q:3 * q]

    xs = xs_scr[...]
    xdt = xs * dt_x
    xdt_b = xdt.astype(BF16)
    xdec_b = (xdt * dec_x).astype(BF16)

    gw = SSM_HEADS_PER_GROUP * SSM_HEAD_DIM
    for g in range(SSM_GROUPS):
        gcols = slice(g * gw, (g + 1) * gw)
        ncols = slice(g * SSM_STATE, (g + 1) * SSM_STATE)
        bg = b_scr[:, ncols]
        cg = c_scr[:, ncols]
        cb = lax.dot_general(cg, bg, (((1,), (1,)), ((), ())),
                             preferred_element_type=F32)
        st = state_scr[:, gcols]
        y_off = jnp.dot(cg, st.astype(BF16),
                        preferred_element_type=F32) * ecs_x[:, gcols]
        m_parts = []
        for j in range(SSM_HEADS_PER_GROUP):
            h = g * SSM_HEADS_PER_GROUP + j
            seg = cs2[:, h:h + 1] - cs2_t[h:h + 1, :]
            lmat = jnp.exp2(jnp.where(causal, seg, -jnp.inf))
            m_parts.append((cb * lmat).astype(BF16))
            bd_scr[g, j * q:(j + 1) * q, j * SSM_HEAD_DIM:(j + 1) * SSM_HEAD_DIM] = (
                xdt_b[:, h * SSM_HEAD_DIM:(h + 1) * SSM_HEAD_DIM])
        y_g = jnp.dot(jnp.concatenate(m_parts, axis=1), bd_scr[g],
                      preferred_element_type=F32) + y_off
        upd = lax.dot_general(bg, xdec_b[:, gcols], (((0,), (0,)), ((), ())),
                              preferred_element_type=F32)
        state_scr[:, gcols] = st * ecs_x[q - 1:q, gcols] + upd

        y_g = y_g + xs[:, gcols] * dskip_ref[:, gcols]
        yg = y_g * _half_silu(z_ref[:, gcols].astype(F32))
        ms = jnp.mean(yg * yg, axis=-1, keepdims=True)
        yg = yg * lax.rsqrt(ms + NORM_EPS) * nw_ref[:, gcols]
        o_ref[:, gcols] = yg.astype(BF16)


def _shift_matrix():
    q = SSM_CHUNK
    win = q + CONV_HALO
    m = np.zeros((q, SSM_CONV * win), np.float32)
    for k in range(SSM_CONV):
        for t in range(q):
            m[t, k * win + CONV_HALO + t - (SSM_CONV - 1) + k] = 1.0
    return jnp.asarray(m, BF16)


def _cumsum_matrix():
    tri = np.tril(np.ones((SSM_CHUNK, SSM_CHUNK), np.float32))
    return jnp.asarray(np.concatenate([tri, tri, tri], axis=1), BF16)


def _ssd(proj3, dt3, cw, cb, dtb, aneg, dskip_x, nw, expand_mat, tri, shift):
    bsz, s, _ = proj3.shape
    q = SSM_CHUNK
    rows = SSD_SUB * q
    const = lambda b, c: (0, 0)
    return pl.pallas_call(
        _ssd_body,
        grid=(bsz, s // rows),
        in_specs=[
            pl.BlockSpec((1, rows, SSM_CONV_DIM), lambda b, c: (b, c, COL_XBC // SSM_CONV_DIM)),
            pl.BlockSpec((1, rows, SSM_INNER), lambda b, c: (b, c, COL_Z // SSM_INNER)),
            pl.BlockSpec((1, rows, LANES), lambda b, c: (b, c, 0)),
            pl.BlockSpec((SSM_CONV, SSM_CONV_DIM), const),
            pl.BlockSpec((1, SSM_CONV_DIM), const),
            pl.BlockSpec((1, LANES), const),
            pl.BlockSpec((1, LANES), const),
            pl.BlockSpec((1, SSM_INNER), const),
            pl.BlockSpec((1, SSM_INNER), const),
            pl.BlockSpec((2 * LANES, SSM_INNER), const),
            pl.BlockSpec((q, 3 * q), const),
            pl.BlockSpec((q, SSM_CONV * (q + CONV_HALO)), const),
        ],
        out_specs=pl.BlockSpec((1, rows, SSM_INNER), lambda b, c: (b, c, 0)),
        out_shape=jax.ShapeDtypeStruct((bsz, s, SSM_INNER), BF16),
        scratch_shapes=[
            pltpu.VMEM((q, SSM_CONV_DIM), BF16),
            pltpu.VMEM((rows, SSM_INNER), F32),
            pltpu.VMEM((rows, SSM_GROUPS * SSM_STATE), BF16),
            pltpu.VMEM((rows, SSM_GROUPS * SSM_STATE), BF16),
            pltpu.VMEM((SSD_SUB, SSM_GROUPS, SSM_HEADS_PER_GROUP * q,
                        SSM_HEADS_PER_GROUP * SSM_HEAD_DIM), BF16),
            pltpu.VMEM((SSM_STATE, SSM_INNER), F32),
        ],
        compiler_params=pltpu.CompilerParams(
            dimension_semantics=("arbitrary", "arbitrary"),
            vmem_limit_bytes=VMEM_LIMIT),
        name="ssd",
    )(proj3, proj3, dt3, cw, cb, dtb, aneg, dskip_x, nw, expand_mat, tri, shift)


def _attn_block(q, k, v, bias):
    nq, nk = bias.shape
    low = lax.broadcasted_iota(jnp.int32, (nq, LANES), 1) < ATTN_HEAD_DIM
    q2 = jnp.concatenate([jnp.where(low, q, 0.0), jnp.where(low, 0.0, q)],
                         axis=0).astype(BF16)
    s = lax.dot_general(q2, k.astype(BF16), (((1,), (1,)), ((), ())),
                        preferred_element_type=F32)
    s = s + jnp.concatenate([bias, bias], axis=0)
    mx = jnp.max(s, axis=-1, keepdims=True)
    p = jnp.exp2(s - mx).astype(BF16)
    vext = jnp.concatenate([v.astype(BF16), jnp.ones((nk, LANES), BF16)], axis=1)
    r = jnp.dot(p, vext, preferred_element_type=F32)
    acc = jnp.where(low, r[0:nq, 0:LANES], r[nq:, 0:LANES])
    den = jnp.where(low, r[0:nq, LANES:], r[nq:, LANES:])
    mxp = jnp.where(low, mx[0:nq], mx[nq:])
    return acc / den, mxp + jnp.log2(den)


def _attn_body(qa_ref, qb_ref, kv_ref, out_ref,
               o4_scr, l4_scr, o16_scr, l16_scr, bias_scr, wbias_scr):
    blk = ATTN_BLOCK
    seq = kv_ref.shape[1]
    (w1, d1), (w4, d4), (w16, d16) = ATTN_PATTERNS
    n_back = w1 // d1
    assert n_back == w4 // d4 == w16 // d16 == blk

    qi = lax.broadcasted_iota(jnp.int32, (blk, 2 * blk), 0) + blk
    kj = lax.broadcasted_iota(jnp.int32, (blk, 2 * blk), 1)
    diff = qi - kj
    band = (diff >= 0) & (diff <= n_back)
    bias_scr[0] = jnp.where(band & (kj >= blk), 0.0, -jnp.inf)
    bias_scr[1] = jnp.where(band, 0.0, -jnp.inf)
    wdiff = (lax.broadcasted_iota(jnp.int32, (2 * blk, 2 * blk), 0)
             - lax.broadcasted_iota(jnp.int32, (2 * blk, 2 * blk), 1))
    wbias_scr[...] = jnp.where((wdiff >= 0) & (wdiff <= n_back), 0.0, -jnp.inf)

    def whole_phase(q_ref, unpack_q, o_scr, l_scr, dil):
        length = seq // dil
        assert length == 2 * blk

        def body(r, carry):
            rows = pl.ds(r, length, stride=dil)
            kv = kv_ref[0, rows, :]
            o, lse = _attn_block(unpack_q(q_ref[0, rows, :]), _unpack_high(kv), _unpack_low(kv),
                                 wbias_scr[...])
            o_scr[rows, :] = o
            l_scr[rows, :] = lse
            return carry

        lax.fori_loop(0, dil, body, 0, unroll=ATTN_UNROLL // 2)

    def strided_phase(q_ref, unpack_q, o_scr, l_scr, dil):
        span = blk * dil

        def body(idx, carry):
            r = idx % dil
            n = idx // dil
            cur = r + span * n
            prv = jnp.maximum(cur - span, r)

            def ld(ref, start):
                return ref[0, pl.ds(start, blk, stride=dil), :]

            kv = jnp.concatenate([ld(kv_ref, prv), ld(kv_ref, cur)], axis=0)
            o, lse = _attn_block(unpack_q(ld(q_ref, cur)), _unpack_high(kv), _unpack_low(kv),
                                 bias_scr[jnp.minimum(n, 1)])
            o_scr[pl.ds(cur, blk, stride=dil), :] = o
            l_scr[pl.ds(cur, blk, stride=dil), :] = lse
            return carry

        lax.fori_loop(0, seq // blk, body, 0, unroll=ATTN_UNROLL)

    whole_phase(qb_ref, _unpack_high, o16_scr, l16_scr, d16)
    strided_phase(qa_ref, _unpack_low, o4_scr, l4_scr, d4)

    def body(n, carry):
        cur = pl.multiple_of(n * blk, blk)
        prv = pl.multiple_of(jnp.maximum(cur - blk, 0), blk)
        rows = pl.ds(cur, blk)
        prows = pl.ds(prv, blk)
        kv = jnp.concatenate([kv_ref[0, prows, :], kv_ref[0, rows, :]], axis=0)
        o1, l1 = _attn_block(_unpack_high(qa_ref[0, rows, :]), _unpack_high(kv), _unpack_low(kv),
                             bias_scr[jnp.minimum(n, 1)])
        l4, l16 = l4_scr[rows, :], l16_scr[rows, :]
        lm = jnp.maximum(jnp.maximum(l1, l4), l16)
        e1, e4, e16 = jnp.exp2(l1 - lm), jnp.exp2(l4 - lm), jnp.exp2(l16 - lm)
        y = (e1 * o1 + e4 * o4_scr[rows, :] + e16 * o16_scr[rows, :]) / (e1 + e4 + e16)
        gate = _half_silu(_unpack_low(qb_ref[0, rows, :]))
        out_ref[0, rows, :] = (y * gate).astype(BF16)
        return carry

    lax.fori_loop(0, seq // blk, body, 0, unroll=ATTN_UNROLL)


def _attention(slabs, bsz, seq):
    slab = lambda s0: (lambda b, p: (s0 + p, b, 0))
    spec = lambda s0: pl.BlockSpec((1, seq, LANES), slab(s0))
    return pl.pallas_call(
        _attn_body,
        grid=(bsz, PAIRS),
        in_specs=[spec(SLAB_Q1_Q4), spec(SLAB_Q16_GATE), spec(SLAB_K_V)],
        out_specs=pl.BlockSpec((1, seq, LANES), lambda b, p: (b, 0, p)),
        out_shape=jax.ShapeDtypeStruct((bsz, seq, ATTN_KV_WIDTH), BF16),
        scratch_shapes=[pltpu.VMEM((seq, LANES), F32) for _ in range(4)]
        + [pltpu.VMEM((2, ATTN_BLOCK, 2 * ATTN_BLOCK), F32),
           pltpu.VMEM((2 * ATTN_BLOCK, 2 * ATTN_BLOCK), F32)],
        compiler_params=pltpu.CompilerParams(
            dimension_semantics=("arbitrary", "arbitrary"),
            vmem_limit_bytes=VMEM_LIMIT),
        name="dilated_attention",
    )(slabs, slabs, slabs)


def _merge_body(x_ref, sc_ref, m0_ref, m1_ref, m2_ref, yssm_ref, yattn_ref,
                scw_ref, pssm_ref, pattn_ref, psc_ref, wout_ref, nw_ref,
                out_ref, ext_scr, *, tiles_per_seq):
    tm = x_ref.shape[1]
    halo = SUBLANES
    first = (pl.program_id(0) % tiles_per_seq) == 0

    @pl.when(first)
    def _():
        ext_scr[0:halo, :] = jnp.zeros((halo, SC_WIDTH), F32)

    @pl.when(jnp.logical_not(first))
    def _():
        ext_scr[0:halo, :] = ext_scr[tm:tm + halo, :]

    u = sc_ref[0, :, 0:SC_WIDTH].astype(F32)
    b_sc = sc_ref[0, :, SC_WIDTH:2 * SC_WIDTH].astype(F32)
    c_sc = sc_ref[0, :, 2 * SC_WIDTH:3 * SC_WIDTH].astype(F32)
    g_sc = sc_ref[0, :, 3 * SC_WIDTH:4 * SC_WIDTH].astype(F32)
    ext_scr[halo:halo + tm, :] = c_sc * u
    conv = jnp.zeros((tm, SC_WIDTH), F32)
    for k in range(SC_CONV):
        lo = halo - (SC_CONV - 1) + k
        conv = conv + scw_ref[k:k + 1, :] * ext_scr[lo:lo + tm, :]
    y_sc = (b_sc * conv * _half_silu(g_sc)).astype(BF16)

    merged = (
        _half_gate2(m0_ref[0].astype(F32))
        * jnp.dot(yssm_ref[0], pssm_ref[...], preferred_element_type=F32)
        + _half_gate2(m1_ref[0].astype(F32))
        * jnp.dot(yattn_ref[0], pattn_ref[...], preferred_element_type=F32)
        + _half_gate2(m2_ref[0].astype(F32))
        * jnp.dot(y_sc, psc_ref[...], preferred_element_type=F32))
    out = jnp.dot(merged.astype(BF16), wout_ref[...], preferred_element_type=F32)
    ms = jnp.mean(out * out, axis=-1, keepdims=True)
    out_ref[0] = x_ref[0] + out * lax.rsqrt(ms + NORM_EPS) * nw_ref[...]


def _merge(x, proj, y_ssm, y_attn, scw, w_stack, nw, tm=512):
    bsz, seq, _ = x.shape
    tps = seq // tm
    row = lambda c: (lambda i: (i // tps, i % tps, c))
    const = lambda i: (0, 0)
    wrows = lambda r: (lambda i: (r, 0))
    wa = ATTN_KV_WIDTH
    assert SSM_INNER == D_MODEL == 2 * wa and wa == SC_WIDTH
    return pl.pallas_call(
        functools.partial(_merge_body, tiles_per_seq=tps),
        grid=(bsz * tps,),
        in_specs=[
            pl.BlockSpec((1, tm, D_MODEL), row(0)),
            pl.BlockSpec((1, tm, 4 * SC_WIDTH), row(COL_SC // (4 * SC_WIDTH))),
            pl.BlockSpec((1, tm, D_MODEL), row(COL_MERGE // D_MODEL)),
            pl.BlockSpec((1, tm, D_MODEL), row(COL_MERGE // D_MODEL + 1)),
            pl.BlockSpec((1, tm, D_MODEL), row(COL_MERGE // D_MODEL + 2)),
            pl.BlockSpec((1, tm, SSM_INNER), row(0)),
            pl.BlockSpec((1, tm, wa), row(0)),
            pl.BlockSpec((SC_CONV, SC_WIDTH), const),
            pl.BlockSpec((SSM_INNER, D_MODEL), wrows(0)),
            pl.BlockSpec((wa, D_MODEL), wrows(2)),
            pl.BlockSpec((SC_WIDTH, D_MODEL), wrows(3)),
            pl.BlockSpec((D_MODEL, D_MODEL), wrows(2)),
            pl.BlockSpec((1, D_MODEL), const),
        ],
        out_specs=pl.BlockSpec((1, tm, D_MODEL), row(0)),
        out_shape=jax.ShapeDtypeStruct((bsz, seq, D_MODEL), F32),
        scratch_shapes=[pltpu.VMEM((tm + 2 * SUBLANES, SC_WIDTH), F32)],
        compiler_params=pltpu.CompilerParams(
            dimension_semantics=("arbitrary",),
            vmem_limit_bytes=VMEM_LIMIT),
        name="merge_out",
    )(x, proj, proj, proj, proj, y_ssm, y_attn,
      scw, w_stack, w_stack, w_stack, w_stack, nw)


def _expand_matrix(n_heads, width):
    m = np.zeros((2 * LANES, n_heads * width), np.float32)
    for h in range(n_heads):
        m[h, h * width:(h + 1) * width] = 1.0
        m[LANES + h, h * width:(h + 1) * width] = 1.0
    return jnp.asarray(m, BF16)


def _pad_lanes(v):
    return jnp.pad(v.astype(F32), (0, LANES - v.shape[0]))[None, :]


def _prepare_w_in(w_in):
    o_z, o_xbc, o_dt, o_q, o_k, o_v, o_gat, o_sc, o_mg, o_end = np.cumsum(
        [0, SSM_INNER, SSM_CONV_DIM, SSM_HEADS, 3 * ATTN_KV_WIDTH, ATTN_KV_WIDTH,
         ATTN_KV_WIDTH, ATTN_KV_WIDTH, 4 * SC_WIDTH, 3 * D_MODEL])
    tiles = lambda start, width: [int(start) + c for c in range(0, width, PREP_TILE)]
    qkv_scale = np.ones((QKV_WIDTH,), np.float32)
    qkv_scale[QCOL_Q1:QCOL_K] = ATTN_HEAD_DIM ** -0.5 * LOG2E
    qkv_scale[QCOL_GATE:] = 0.5
    w_qkv = _prep_weights(w_in, tiles(o_q, QKV_WIDTH), qkv_scale, "prep_w_qkv")
    perm_scale = np.ones((PROJ_WIDTH,), np.float32)
    perm_scale[COL_SC + 3 * SC_WIDTH:] = 0.5
    perm_cols = (tiles(o_xbc, SSM_CONV_DIM) + tiles(o_sc, 4 * SC_WIDTH)
                 + tiles(o_z, SSM_INNER) + tiles(o_mg, 3 * D_MODEL))
    w_perm = _prep_weights(w_in, perm_cols, perm_scale, "prep_w_proj")
    return w_qkv, w_perm


def kernel(x, norm_pre, norm_post, w_in, ssm_conv_w, ssm_conv_b, dt_bias, a_log,
           d_skip, ssm_norm, sc_conv_w, p_ssm, p_attn, p_sc, w_out):
    bsz, seq, _ = x.shape
    expand_ssm = _expand_matrix(SSM_HEADS, SSM_HEAD_DIM)
    cumsum_mat = _cumsum_matrix()
    shift = _shift_matrix()
    w_qkv, w_perm = _prepare_w_in(w_in)
    o_dt = SSM_INNER + SSM_CONV_DIM
    for i in range(norm_pre.shape[0]):
        w_dt = jnp.pad(w_in[i][:, o_dt:o_dt + SSM_HEADS],
                       ((0, 0), (0, LANES - SSM_HEADS))).astype(BF16)
        nw_pre = norm_pre[i].reshape(1, D_MODEL)
        conv_w = (0.5 * ssm_conv_w[i]).astype(BF16)
        conv_b = (0.5 * ssm_conv_b[i]).reshape(1, SSM_CONV_DIM)
        dskip_x = jnp.repeat(d_skip[i].astype(F32), SSM_HEAD_DIM).reshape(1, SSM_INNER)

        slabs = _qkv_proj(x, nw_pre, w_qkv, i)
        proj, dt_raw = _in_proj(x, nw_pre, w_perm, i, w_dt)
        y_ssm = _ssd(proj, dt_raw, conv_w, conv_b, _pad_lanes(dt_bias[i]),
                     _pad_lanes(-jnp.exp(a_log[i].astype(F32))), dskip_x,
                     ssm_norm[i].reshape(1, SSM_INNER), expand_ssm, cumsum_mat, shift)
        y_attn = _attention(slabs, bsz, seq)
        w_stack = jnp.concatenate(
            [0.5 * p_ssm[i], 0.5 * p_attn[i], 0.5 * p_sc[i], w_out[i]], axis=0).astype(BF16)
        x = _merge(x, proj, y_ssm, y_attn, sc_conv_w[i], w_stack,
                   norm_post[i].reshape(1, D_MODEL))
    return x
```

```python
import functools

import jax
import jax.numpy as jnp
import numpy as np
from jax import lax
from jax.experimental import pallas as pl
from jax.experimental.pallas import tpu as pltpu

F32 = jnp.float32
BF16 = jnp.bfloat16

D_MODEL = 1024
SSM_HEADS = 16
SSM_HEAD_DIM = 64
SSM_INNER = SSM_HEADS * SSM_HEAD_DIM
SSM_GROUPS = 4
SSM_HEADS_PER_GROUP = SSM_HEADS // SSM_GROUPS
SSM_STATE = 128
SSM_CONV = 4
SSM_CHUNK = 128
SSM_CONV_DIM = SSM_INNER + 2 * SSM_GROUPS * SSM_STATE
ATTN_HEAD_DIM = 64
ATTN_SLOTS = 8
ATTN_PATTERNS = ((128, 1), (512, 4), (2048, 16))
ATTN_KV_WIDTH = ATTN_SLOTS * ATTN_HEAD_DIM
ATTN_BLOCK = 128
SC_WIDTH = 512
SC_CONV = 3
SSD_SUB = 8
ATTN_UNROLL = 8
NORM_EPS = 1e-6
LOG2E = float(np.log2(np.e))

LANES = 128
SUBLANES = 8
CONV_HALO = 2 * SUBLANES
VMEM_LIMIT = 48 * 1024 * 1024

COL_XBC = 0
COL_SC = COL_XBC + SSM_CONV_DIM
COL_Z = COL_SC + 4 * SC_WIDTH
COL_MERGE = COL_Z + SSM_INNER
PROJ_WIDTH = COL_MERGE + 3 * D_MODEL

PAIRS = ATTN_KV_WIDTH // LANES
SLAB_Q1_Q4 = 0
SLAB_Q16_GATE = SLAB_Q1_Q4 + PAIRS
SLAB_K_V = SLAB_Q16_GATE + PAIRS
N_SLABS = SLAB_K_V + PAIRS
QKV_WIDTH = 2 * N_SLABS * LANES
QCOL_Q1 = 0
QCOL_Q4 = QCOL_Q1 + ATTN_KV_WIDTH
QCOL_Q16 = QCOL_Q4 + ATTN_KV_WIDTH
QCOL_K = QCOL_Q16 + ATTN_KV_WIDTH
QCOL_V = QCOL_K + ATTN_KV_WIDTH
QCOL_GATE = QCOL_V + ATTN_KV_WIDTH
HIGH_HALF = -65536


def _half_silu(hx):
    return hx * jnp.tanh(hx) + hx


def _half_gate2(hx):
    return jnp.tanh(hx) + 1.0


def _pack_pair(a, b):
    ab = lax.bitcast_convert_type(a.astype(BF16).astype(F32), jnp.int32)
    bb = lax.bitcast_convert_type(b.astype(BF16).astype(F32), jnp.int32)
    return ab | lax.shift_right_logical(bb, 16)


def _unpack_high(w):
    return lax.bitcast_convert_type(w & HIGH_HALF, F32)


def _unpack_low(w):
    return lax.bitcast_convert_type(lax.shift_left(w, 16), F32)


def _split_hi_lo(v):
    hi = v.astype(BF16)
    lo = (v - hi.astype(F32)).astype(BF16)
    return jnp.concatenate([hi, lo], axis=1)


def _normed(x_ref, nw_ref):
    x = x_ref[0]
    ms = jnp.mean(x * x, axis=-1, keepdims=True)
    return (x * lax.rsqrt(ms + NORM_EPS) * nw_ref[...]).astype(BF16)


PREP_TILE = 1024


def _wprep_body(base_ref, shift_ref, main_ref, next_ref, scale_ref, o_ref):
    main = main_ref[...]
    both = jnp.concatenate([main, next_ref[...]], axis=1)
    moved = both[:, SSM_HEADS:SSM_HEADS + PREP_TILE]
    src = jnp.where(shift_ref[pl.program_id(1)] > 0, moved, main)
    o_ref[...] = (src * scale_ref[...]).astype(BF16)


def _prep_weights(w_in, src_cols, col_scale, name):
    depth = w_in.shape[0]
    n_tiles = len(src_cols)
    assert all(c % PREP_TILE in (0, SSM_HEADS) for c in src_cols)
    base = jnp.asarray([c // PREP_TILE for c in src_cols], jnp.int32)
    shift = jnp.asarray([c % PREP_TILE for c in src_cols], jnp.int32)
    per_tile = PREP_TILE // LANES
    grid_spec = pltpu.PrefetchScalarGridSpec(
        num_scalar_prefetch=2,
        grid=(depth, n_tiles),
        in_specs=[
            pl.BlockSpec((None, D_MODEL, PREP_TILE), lambda l, t, b, s: (l, 0, b[t])),
            pl.BlockSpec((None, D_MODEL, LANES), lambda l, t, b, s: (l, 0, (b[t] + 1) * per_tile)),
            pl.BlockSpec((1, PREP_TILE), lambda l, t, b, s: (0, t)),
        ],
        out_specs=pl.BlockSpec((None, D_MODEL, PREP_TILE), lambda l, t, b, s: (l, 0, t)),
    )
    return pl.pallas_call(
        _wprep_body,
        grid_spec=grid_spec,
        out_shape=jax.ShapeDtypeStruct((depth, D_MODEL, n_tiles * PREP_TILE), BF16),
        compiler_params=pltpu.CompilerParams(
            dimension_semantics=("arbitrary", "arbitrary"),
            vmem_limit_bytes=VMEM_LIMIT),
        name=name,
    )(base, shift, w_in, w_in, jnp.asarray(col_scale, F32).reshape(1, -1))


def _qkvproj_body(x_ref, nw_ref, w_ref, o_ref):
    res = jnp.dot(_normed(x_ref, nw_ref), w_ref[...], preferred_element_type=F32)
    for first, (hi_col, lo_col) in ((SLAB_Q1_Q4, (QCOL_Q1, QCOL_Q4)),
                                    (SLAB_Q16_GATE, (QCOL_Q16, QCOL_GATE)),
                                    (SLAB_K_V, (QCOL_K, QCOL_V))):
        for p in range(PAIRS):
            hi = res[:, hi_col + p * LANES:hi_col + (p + 1) * LANES]
            lo = res[:, lo_col + p * LANES:lo_col + (p + 1) * LANES]
            o_ref[first + p] = _pack_pair(hi, lo)


def _qkv_proj(x, norm_w, w_qkv, layer, tm=512):
    bsz, seq, _ = x.shape
    tps = seq // tm
    return pl.pallas_call(
        _qkvproj_body,
        grid=(bsz * tps,),
        in_specs=[
            pl.BlockSpec((1, tm, D_MODEL), lambda i: (i // tps, i % tps, 0)),
            pl.BlockSpec((1, D_MODEL), lambda i: (0, 0)),
            pl.BlockSpec((None, D_MODEL, QKV_WIDTH), lambda i: (layer, 0, 0)),
        ],
        out_specs=pl.BlockSpec((N_SLABS, tm, LANES), lambda i: (0, i, 0)),
        out_shape=jax.ShapeDtypeStruct((N_SLABS, bsz * seq, LANES), jnp.int32),
        compiler_params=pltpu.CompilerParams(
            dimension_semantics=("arbitrary",),
            vmem_limit_bytes=VMEM_LIMIT),
        name="qkv_proj",
    )(x, norm_w, w_qkv)


def _inproj_body(x_ref, nw_ref, w_ref, wdt_ref, o_ref, dt_ref, h_scr):
    @pl.when(pl.program_id(1) == 0)
    def _():
        h = _normed(x_ref, nw_ref)
        h_scr[...] = h
        dt_ref[0] = jnp.dot(h, wdt_ref[...], preferred_element_type=F32)

    o_ref[0] = jnp.dot(h_scr[...], w_ref[...], preferred_element_type=F32).astype(BF16)


def _in_proj(x, norm_w, w_perm, w_qkv, layer, tm=1024, tn=2048):
    bsz, seq, _ = x.shape
    tps = seq // tm
    return pl.pallas_call(
        _inproj_body,
        grid=(bsz * tps, PROJ_WIDTH // tn),
        in_specs=[
            pl.BlockSpec((1, tm, D_MODEL), lambda i, j: (i // tps, i % tps, 0)),
            pl.BlockSpec((1, D_MODEL), lambda i, j: (0, 0)),
            pl.BlockSpec((None, D_MODEL, tn), lambda i, j: (layer, 0, j)),
            pl.BlockSpec((None, D_MODEL, LANES), lambda i, j: (layer, 0, QKV_WIDTH // LANES)),
        ],
        out_specs=[
            pl.BlockSpec((1, tm, tn), lambda i, j: (i // tps, i % tps, j)),
            pl.BlockSpec((1, tm, LANES), lambda i, j: (i // tps, i % tps, 0)),
        ],
        out_shape=[
            jax.ShapeDtypeStruct((bsz, seq, PROJ_WIDTH), BF16),
            jax.ShapeDtypeStruct((bsz, seq, LANES), F32),
        ],
        scratch_shapes=[pltpu.VMEM((tm, D_MODEL), BF16)],
        compiler_params=pltpu.CompilerParams(
            dimension_semantics=("arbitrary", "arbitrary"),
            vmem_limit_bytes=VMEM_LIMIT),
        name="in_proj",
    )(x, norm_w, w_perm, w_qkv)


def _ssd_body(xbc_ref, z_ref, dt_ref, cw_ref, cb_ref, dtb_ref, aneg_ref,
              dskip_ref, nw_ref, expand_ref, tri_ref, shift_ref, o_ref,
              prev_scr, xs_scr, b_scr, c_scr, bd_scr, state_scr):
    q = SSM_CHUNK

    @pl.when(pl.program_id(1) == 0)
    def _():
        prev_scr[...] = jnp.zeros_like(prev_scr)
        state_scr[...] = jnp.zeros_like(state_scr)
        bd_scr[...] = jnp.zeros_like(bd_scr)

    row = lax.broadcasted_iota(jnp.int32, (q, q), 0)
    col = lax.broadcasted_iota(jnp.int32, (q, q), 1)
    causal = row >= col

    for sub in range(SSD_SUB):
        rows = slice(sub * q, (sub + 1) * q)
        _ssd_chunk(
            xbc_ref.at[0, rows, :],
            prev_scr if sub == 0 else xbc_ref.at[0, slice((sub - 1) * q, sub * q), :],
            z_ref.at[0, rows, :], dt_ref.at[0, rows, :], cw_ref, cb_ref, dtb_ref, aneg_ref,
            dskip_ref, nw_ref, expand_ref, tri_ref, shift_ref, o_ref.at[0, rows, :],
            xs_scr.at[rows, :], b_scr.at[rows, :], c_scr.at[rows, :], bd_scr.at[sub],
            state_scr, causal)
    prev_scr[...] = xbc_ref[0, (SSD_SUB - 1) * q:SSD_SUB * q, :]


def _ssd_chunk(xbc_ref, prev_ref, z_ref, dt_ref, cw_ref, cb_ref, dtb_ref, aneg_ref,
               dskip_ref, nw_ref, expand_ref, tri_ref, shift_ref, o_ref,
               xs_scr, b_scr, c_scr, bd_scr, state_scr, causal):
    q = SSM_CHUNK

    piece = 512
    for p in range(SSM_CONV_DIM // piece):
        cols = slice(p * piece, (p + 1) * piece)
        win = jnp.concatenate([prev_ref[q - CONV_HALO:q, cols], xbc_ref[:, cols]], axis=0)
        taps = jnp.concatenate([win * cw_ref[k:k + 1, cols] for k in range(SSM_CONV)], axis=0)
        acc = jnp.dot(shift_ref[...], taps, preferred_element_type=F32) + cb_ref[:, cols]
        act = _half_silu(acc)
        if p < 2:
            xs_scr[:, cols] = act
        elif p == 2:
            b_scr[...] = act.astype(BF16)
        else:
            c_scr[...] = act.astype(BF16)

    dt_in = dt_ref[...] + dtb_ref[...]
    dt = jnp.maximum(dt_in, 0.0) + jnp.log(1.0 + jnp.exp(-jnp.abs(dt_in)))
    a = dt * aneg_ref[...]
    a_hi = a.astype(BF16)
    r1 = a - a_hi.astype(F32)
    a_mid = r1.astype(BF16)
    a_lo = (r1 - a_mid.astype(F32)).astype(BF16)
    cs = jnp.dot(tri_ref[...], jnp.concatenate([a_hi, a_mid, a_lo], axis=0),
                 preferred_element_type=F32)
    cs2 = cs * LOG2E
    cs2_t = cs2.T
    ecs = jnp.exp(cs)
    dec = jnp.exp(cs[q - 1:q, :] - cs)
    stacked = jnp.concatenate(
        [_split_hi_lo(dt), _split_hi_lo(ecs), _split_hi_lo(dec)], axis=0)
    expanded = jnp.dot(stacked, expand_ref[...], preferred_element_type=F32)
    dt_x = expanded[0:q]
    ecs_x = expanded[q:2 * q]
    dec_x = expanded[2 * q:3 * q]

    xs = xs_scr[...]
    xdt = xs * dt_x
    xdt_b = xdt.astype(BF16)
    xdec_b = (xdt * dec_x).astype(BF16)

    gw = SSM_HEADS_PER_GROUP * SSM_HEAD_DIM
    for g in range(SSM_GROUPS):
        gcols = slice(g * gw, (g + 1) * gw)
        ncols = slice(g * SSM_STATE, (g + 1) * SSM_STATE)
        bg = b_scr[:, ncols]
        cg = c_scr[:, ncols]
        cb = lax.dot_general(cg, bg, (((1,), (1,)), ((), ())),
                             preferred_element_type=F32)
        st = state_scr[:, gcols]
        y_off = jnp.dot(cg, st.astype(BF16),
                        preferred_element_type=F32) * ecs_x[:, gcols]
        m_parts = []
        for j in range(SSM_HEADS_PER_GROUP):
            h = g * SSM_HEADS_PER_GROUP + j
            seg = cs2[:, h:h + 1] - cs2_t[h:h + 1, :]
            lmat = jnp.exp2(jnp.where(causal, seg, -jnp.inf))
            m_parts.append((cb * lmat).astype(BF16))
            bd_scr[g, j * q:(j + 1) * q, j * SSM_HEAD_DIM:(j + 1) * SSM_HEAD_DIM] = (
                xdt_b[:, h * SSM_HEAD_DIM:(h + 1) * SSM_HEAD_DIM])
        y_g = jnp.dot(jnp.concatenate(m_parts, axis=1), bd_scr[g],
                      preferred_element_type=F32) + y_off
        upd = lax.dot_general(bg, xdec_b[:, gcols], (((0,), (0,)), ((), ())),
                              preferred_element_type=F32)
        state_scr[:, gcols] = st * ecs_x[q - 1:q, gcols] + upd

        y_g = y_g + xs[:, gcols] * dskip_ref[:, gcols]
        yg = y_g * _half_silu(z_ref[:, gcols].astype(F32))
        ms = jnp.mean(yg * yg, axis=-1, keepdims=True)
        yg = yg * lax.rsqrt(ms + NORM_EPS) * nw_ref[:, gcols]
        o_ref[:, gcols] = yg.astype(BF16)


def _shift_matrix():
    q = SSM_CHUNK
    win = q + CONV_HALO
    m = np.zeros((q, SSM_CONV * win), np.float32)
    for k in range(SSM_CONV):
        for t in range(q):
            m[t, k * win + CONV_HALO + t - (SSM_CONV - 1) + k] = 1.0
    return jnp.asarray(m, BF16)


def _cumsum_matrix():
    tri = np.tril(np.ones((SSM_CHUNK, SSM_CHUNK), np.float32))
    return jnp.asarray(np.concatenate([tri, tri, tri], axis=1), BF16)


def _ssd(proj3, dt3, cw, cb, dtb, aneg, dskip_x, nw, expand_mat, tri, shift):
    bsz, s, _ = proj3.shape
    q = SSM_CHUNK
    rows = SSD_SUB * q
    const = lambda b, c: (0, 0)
    return pl.pallas_call(
        _ssd_body,
        grid=(bsz, s // rows),
        in_specs=[
            pl.BlockSpec((1, rows, SSM_CONV_DIM), lambda b, c: (b, c, COL_XBC // SSM_CONV_DIM)),
            pl.BlockSpec((1, rows, SSM_INNER), lambda b, c: (b, c, COL_Z // SSM_INNER)),
            pl.BlockSpec((1, rows, LANES), lambda b, c: (b, c, 0)),
            pl.BlockSpec((SSM_CONV, SSM_CONV_DIM), const),
            pl.BlockSpec((1, SSM_CONV_DIM), const),
            pl.BlockSpec((1, LANES), const),
            pl.BlockSpec((1, LANES), const),
            pl.BlockSpec((1, SSM_INNER), const),
            pl.BlockSpec((1, SSM_INNER), const),
            pl.BlockSpec((2 * LANES, SSM_INNER), const),
            pl.BlockSpec((q, 3 * q), const),
            pl.BlockSpec((q, SSM_CONV * (q + CONV_HALO)), const),
        ],
        out_specs=pl.BlockSpec((1, rows, SSM_INNER), lambda b, c: (b, c, 0)),
        out_shape=jax.ShapeDtypeStruct((bsz, s, SSM_INNER), BF16),
        scratch_shapes=[
            pltpu.VMEM((q, SSM_CONV_DIM), BF16),
            pltpu.VMEM((rows, SSM_INNER), F32),
            pltpu.VMEM((rows, SSM_GROUPS * SSM_STATE), BF16),
            pltpu.VMEM((rows, SSM_GROUPS * SSM_STATE), BF16),
            pltpu.VMEM((SSD_SUB, SSM_GROUPS, SSM_HEADS_PER_GROUP * q,
                        SSM_HEADS_PER_GROUP * SSM_HEAD_DIM), BF16),
            pltpu.VMEM((SSM_STATE, SSM_INNER), F32),
        ],
        compiler_params=pltpu.CompilerParams(
            dimension_semantics=("arbitrary", "arbitrary"),
            vmem_limit_bytes=VMEM_LIMIT),
        name="ssd",
    )(proj3, proj3, dt3, cw, cb, dtb, aneg, dskip_x, nw, expand_mat, tri, shift)


def _attn_block(q, k, v, bias):
    nq, nk = bias.shape
    low = lax.broadcasted_iota(jnp.int32, (nq, LANES), 1) < ATTN_HEAD_DIM
    q2 = jnp.concatenate([jnp.where(low, q, 0.0), jnp.where(low, 0.0, q)],
                         axis=0).astype(BF16)
    s = lax.dot_general(q2, k.astype(BF16), (((1,), (1,)), ((), ())),
                        preferred_element_type=F32)
    s = s + jnp.concatenate([bias, bias], axis=0)
    mx = jnp.max(s, axis=-1, keepdims=True)
    p = jnp.exp2(s - mx).astype(BF16)
    vext = jnp.concatenate([v.astype(BF16), jnp.ones((nk, LANES), BF16)], axis=1)
    r = jnp.dot(p, vext, preferred_element_type=F32)
    acc = jnp.where(low, r[0:nq, 0:LANES], r[nq:, 0:LANES])
    den = jnp.where(low, r[0:nq, LANES:], r[nq:, LANES:])
    mxp = jnp.where(low, mx[0:nq], mx[nq:])
    return acc / den, mxp + jnp.log2(den)


def _attn_body(qa_ref, qb_ref, kv_ref, out_ref,
               o4_scr, l4_scr, o16_scr, l16_scr, bias_scr, wbias_scr):
    blk = ATTN_BLOCK
    seq = kv_ref.shape[1]
    (w1, d1), (w4, d4), (w16, d16) = ATTN_PATTERNS
    n_back = w1 // d1
    assert n_back == w4 // d4 == w16 // d16 == blk

    qi = lax.broadcasted_iota(jnp.int32, (blk, 2 * blk), 0) + blk
    kj = lax.broadcasted_iota(jnp.int32, (blk, 2 * blk), 1)
    diff = qi - kj
    band = (diff >= 0) & (diff <= n_back)
    bias_scr[0] = jnp.where(band & (kj >= blk), 0.0, -jnp.inf)
    bias_scr[1] = jnp.where(band, 0.0, -jnp.inf)
    wdiff = (lax.broadcasted_iota(jnp.int32, (2 * blk, 2 * blk), 0)
             - lax.broadcasted_iota(jnp.int32, (2 * blk, 2 * blk), 1))
    wbias_scr[...] = jnp.where((wdiff >= 0) & (wdiff <= n_back), 0.0, -jnp.inf)

    def whole_phase(q_ref, unpack_q, o_scr, l_scr, dil):
        length = seq // dil
        assert length == 2 * blk

        def body(r, carry):
            rows = pl.ds(r, length, stride=dil)
            kv = kv_ref[0, rows, :]
            o, lse = _attn_block(unpack_q(q_ref[0, rows, :]), _unpack_high(kv), _unpack_low(kv),
                                 wbias_scr[...])
            o_scr[rows, :] = o
            l_scr[rows, :] = lse
            return carry

        lax.fori_loop(0, dil, body, 0, unroll=ATTN_UNROLL // 2)

    def strided_phase(q_ref, unpack_q, o_scr, l_scr, dil):
        span = blk * dil

        def body(idx, carry):
            r = idx % dil
            n = idx // dil
            cur = r + span * n
            prv = jnp.maximum(cur - span, r)

            def ld(ref, start):
                return ref[0, pl.ds(start, blk, stride=dil), :]

            kv = jnp.concatenate([ld(kv_ref, prv), ld(kv_ref, cur)], axis=0)
            o, lse = _attn_block(unpack_q(ld(q_ref, cur)), _unpack_high(kv), _unpack_low(kv),
                                 bias_scr[jnp.minimum(n, 1)])
            o_scr[pl.ds(cur, blk, stride=dil), :] = o
            l_scr[pl.ds(cur, blk, stride=dil), :] = lse
            return carry

        lax.fori_loop(0, seq // blk, body, 0, unroll=ATTN_UNROLL)

    whole_phase(qb_ref, _unpack_high, o16_scr, l16_scr, d16)
    strided_phase(qa_ref, _unpack_low, o4_scr, l4_scr, d4)

    def body(n, carry):
        cur = pl.multiple_of(n * blk, blk)
        prv = pl.multiple_of(jnp.maximum(cur - blk, 0), blk)
        rows = pl.ds(cur, blk)
        prows = pl.ds(prv, blk)
        kv = jnp.concatenate([kv_ref[0, prows, :], kv_ref[0, rows, :]], axis=0)
        o1, l1 = _attn_block(_unpack_high(qa_ref[0, rows, :]), _unpack_high(kv), _unpack_low(kv),
                             bias_scr[jnp.minimum(n, 1)])
        l4, l16 = l4_scr[rows, :], l16_scr[rows, :]
        lm = jnp.maximum(jnp.maximum(l1, l4), l16)
        e1, e4, e16 = jnp.exp2(l1 - lm), jnp.exp2(l4 - lm), jnp.exp2(l16 - lm)
        y = (e1 * o1 + e4 * o4_scr[rows, :] + e16 * o16_scr[rows, :]) / (e1 + e4 + e16)
        gate = _half_silu(_unpack_low(qb_ref[0, rows, :]))
        out_ref[0, rows, :] = (y * gate).astype(BF16)
        return carry

    lax.fori_loop(0, seq // blk, body, 0, unroll=ATTN_UNROLL)


def _attention(slabs, bsz, seq):
    slab = lambda s0: (lambda b, p: (s0 + p, b, 0))
    spec = lambda s0: pl.BlockSpec((1, seq, LANES), slab(s0))
    return pl.pallas_call(
        _attn_body,
        grid=(bsz, PAIRS),
        in_specs=[spec(SLAB_Q1_Q4), spec(SLAB_Q16_GATE), spec(SLAB_K_V)],
        out_specs=pl.BlockSpec((1, seq, LANES), lambda b, p: (b, 0, p)),
        out_shape=jax.ShapeDtypeStruct((bsz, seq, ATTN_KV_WIDTH), BF16),
        scratch_shapes=[pltpu.VMEM((seq, LANES), F32) for _ in range(4)]
        + [pltpu.VMEM((2, ATTN_BLOCK, 2 * ATTN_BLOCK), F32),
           pltpu.VMEM((2 * ATTN_BLOCK, 2 * ATTN_BLOCK), F32)],
        compiler_params=pltpu.CompilerParams(
            dimension_semantics=("arbitrary", "arbitrary"),
            vmem_limit_bytes=VMEM_LIMIT),
        name="dilated_attention",
    )(slabs, slabs, slabs)


def _merge_body(x_ref, sc_ref, m0_ref, m1_ref, m2_ref, yssm_ref, yattn_ref,
                scw_ref, pssm_ref, pattn_ref, psc_ref, wout_ref, nw_ref,
                out_ref, ext_scr, *, tiles_per_seq):
    tm = x_ref.shape[1]
    halo = SUBLANES
    first = (pl.program_id(0) % tiles_per_seq) == 0

    @pl.when(first)
    def _():
        ext_scr[0:halo, :] = jnp.zeros((halo, SC_WIDTH), F32)

    @pl.when(jnp.logical_not(first))
    def _():
        ext_scr[0:halo, :] = ext_scr[tm:tm + halo, :]

    u = sc_ref[0, :, 0:SC_WIDTH].astype(F32)
    b_sc = sc_ref[0, :, SC_WIDTH:2 * SC_WIDTH].astype(F32)
    c_sc = sc_ref[0, :, 2 * SC_WIDTH:3 * SC_WIDTH].astype(F32)
    g_sc = sc_ref[0, :, 3 * SC_WIDTH:4 * SC_WIDTH].astype(F32)
    ext_scr[halo:halo + tm, :] = c_sc * u
    conv = jnp.zeros((tm, SC_WIDTH), F32)
    for k in range(SC_CONV):
        lo = halo - (SC_CONV - 1) + k
        conv = conv + scw_ref[k:k + 1, :] * ext_scr[lo:lo + tm, :]
    y_sc = (b_sc * conv * _half_silu(g_sc)).astype(BF16)

    merged = (
        _half_gate2(m0_ref[0].astype(F32))
        * jnp.dot(yssm_ref[0], pssm_ref[...], preferred_element_type=F32)
        + _half_gate2(m1_ref[0].astype(F32))
        * jnp.dot(yattn_ref[0], pattn_ref[...], preferred_element_type=F32)
        + _half_gate2(m2_ref[0].astype(F32))
        * jnp.dot(y_sc, psc_ref[...], preferred_element_type=F32))
    out = jnp.dot(merged.astype(BF16), wout_ref[...], preferred_element_type=F32)
    ms = jnp.mean(out * out, axis=-1, keepdims=True)
    out_ref[0] = x_ref[0] + out * lax.rsqrt(ms + NORM_EPS) * nw_ref[...]


def _merge(x, proj, y_ssm, y_attn, scw, w_stack, nw, tm=512):
    bsz, seq, _ = x.shape
    tps = seq // tm
    row = lambda c: (lambda i: (i // tps, i % tps, c))
    const = lambda i: (0, 0)
    wrows = lambda r: (lambda i: (r, 0))
    wa = ATTN_KV_WIDTH
    assert SSM_INNER == D_MODEL == 2 * wa and wa == SC_WIDTH
    return pl.pallas_call(
        functools.partial(_merge_body, tiles_per_seq=tps),
        grid=(bsz * tps,),
        in_specs=[
            pl.BlockSpec((1, tm, D_MODEL), row(0)),
            pl.BlockSpec((1, tm, 4 * SC_WIDTH), row(COL_SC // (4 * SC_WIDTH))),
            pl.BlockSpec((1, tm, D_MODEL), row(COL_MERGE // D_MODEL)),
            pl.BlockSpec((1, tm, D_MODEL), row(COL_MERGE // D_MODEL + 1)),
            pl.BlockSpec((1, tm, D_MODEL), row(COL_MERGE // D_MODEL + 2)),
            pl.BlockSpec((1, tm, SSM_INNER), row(0)),
            pl.BlockSpec((1, tm, wa), row(0)),
            pl.BlockSpec((SC_CONV, SC_WIDTH), const),
            pl.BlockSpec((SSM_INNER, D_MODEL), wrows(0)),
            pl.BlockSpec((wa, D_MODEL), wrows(2)),
            pl.BlockSpec((SC_WIDTH, D_MODEL), wrows(3)),
            pl.BlockSpec((D_MODEL, D_MODEL), wrows(2)),
            pl.BlockSpec((1, D_MODEL), const),
        ],
        out_specs=pl.BlockSpec((1, tm, D_MODEL), row(0)),
        out_shape=jax.ShapeDtypeStruct((bsz, seq, D_MODEL), F32),
        scratch_shapes=[pltpu.VMEM((tm + 2 * SUBLANES, SC_WIDTH), F32)],
        compiler_params=pltpu.CompilerParams(
            dimension_semantics=("arbitrary",),
            vmem_limit_bytes=VMEM_LIMIT),
        name="merge_out",
    )(x, proj, proj, proj, proj, y_ssm, y_attn,
      scw, w_stack, w_stack, w_stack, w_stack, nw)


def _expand_matrix(n_heads, width):
    m = np.zeros((2 * LANES, n_heads * width), np.float32)
    for h in range(n_heads):
        m[h, h * width:(h + 1) * width] = 1.0
        m[LANES + h, h * width:(h + 1) * width] = 1.0
    return jnp.asarray(m, BF16)


def _pad_lanes(v):
    return jnp.pad(v.astype(F32), (0, LANES - v.shape[0]))[None, :]


def _prepare_w_in(w_in):
    o_z, o_xbc, o_dt, o_q, o_k, o_v, o_gat, o_sc, o_mg, o_end = np.cumsum(
        [0, SSM_INNER, SSM_CONV_DIM, SSM_HEADS, 3 * ATTN_KV_WIDTH, ATTN_KV_WIDTH,
         ATTN_KV_WIDTH, ATTN_KV_WIDTH, 4 * SC_WIDTH, 3 * D_MODEL])
    tiles = lambda start, width: [int(start) + c for c in range(0, width, PREP_TILE)]
    qkv_scale = np.ones((QKV_WIDTH,), np.float32)
    qkv_scale[QCOL_Q1:QCOL_K] = ATTN_HEAD_DIM ** -0.5 * LOG2E
    qkv_scale[QCOL_GATE:] = 0.5
    dt_scale = np.zeros((PREP_TILE,), np.float32)
    dt_scale[:SSM_HEADS] = 1.0
    w_qkv = _prep_weights(w_in, tiles(o_q, QKV_WIDTH) + [int(o_dt)],
                          np.concatenate([qkv_scale, dt_scale]), "prep_w_qkv")
    perm_scale = np.ones((PROJ_WIDTH,), np.float32)
    perm_scale[COL_SC + 3 * SC_WIDTH:] = 0.5
    perm_cols = (tiles(o_xbc, SSM_CONV_DIM) + tiles(o_sc, 4 * SC_WIDTH)
                 + tiles(o_z, SSM_INNER) + tiles(o_mg, 3 * D_MODEL))
    w_perm = _prep_weights(w_in, perm_cols, perm_scale, "prep_w_proj")
    return w_qkv, w_perm


def kernel(x, norm_pre, norm_post, w_in, ssm_conv_w, ssm_conv_b, dt_bias, a_log,
           d_skip, ssm_norm, sc_conv_w, p_ssm, p_attn, p_sc, w_out):
    bsz, seq, _ = x.shape
    expand_ssm = _expand_matrix(SSM_HEADS, SSM_HEAD_DIM)
    cumsum_mat = _cumsum_matrix()
    shift = _shift_matrix()
    w_qkv, w_perm = _prepare_w_in(w_in)
    for i in range(norm_pre.shape[0]):
        nw_pre = norm_pre[i].reshape(1, D_MODEL)
        conv_w = (0.5 * ssm_conv_w[i]).astype(BF16)
        conv_b = (0.5 * ssm_conv_b[i]).reshape(1, SSM_CONV_DIM)
        dskip_x = jnp.repeat(d_skip[i].astype(F32), SSM_HEAD_DIM).reshape(1, SSM_INNER)

        slabs = _qkv_proj(x, nw_pre, w_qkv, i)
        proj, dt_raw = _in_proj(x, nw_pre, w_perm, w_qkv, i)
        y_ssm = _ssd(proj, dt_raw, conv_w, conv_b, _pad_lanes(dt_bias[i]),
                     _pad_lanes(-jnp.exp(a_log[i].astype(F32))), dskip_x,
                     ssm_norm[i].reshape(1, SSM_INNER), expand_ssm, cumsum_mat, shift)
        y_attn = _attention(slabs, bsz, seq)
        w_stack = jnp.concatenate(
            [0.5 * p_ssm[i], 0.5 * p_attn[i], 0.5 * p_sc[i], w_out[i]], axis=0).astype(BF16)
        x = _merge(x, proj, y_ssm, y_attn, sc_conv_w[i], w_stack,
                   norm_post[i].reshape(1, D_MODEL))
    return x
```

```python
import functools

import jax
import jax.numpy as jnp
import numpy as np
from jax import lax
from jax.experimental import pallas as pl
from jax.experimental.pallas import tpu as pltpu

F32 = jnp.float32
BF16 = jnp.bfloat16

D_MODEL = 1024
SSM_HEADS = 16
SSM_HEAD_DIM = 64
SSM_INNER = SSM_HEADS * SSM_HEAD_DIM
SSM_GROUPS = 4
SSM_HEADS_PER_GROUP = SSM_HEADS // SSM_GROUPS
SSM_STATE = 128
SSM_CONV = 4
SSM_CHUNK = 128
SSM_CONV_DIM = SSM_INNER + 2 * SSM_GROUPS * SSM_STATE
ATTN_HEAD_DIM = 64
ATTN_SLOTS = 8
ATTN_PATTERNS = ((128, 1), (512, 4), (2048, 16))
ATTN_KV_WIDTH = ATTN_SLOTS * ATTN_HEAD_DIM
ATTN_BLOCK = 128
SC_WIDTH = 512
SC_CONV = 3
SSD_SUB = 8
ATTN_UNROLL = 8
NORM_EPS = 1e-6
LOG2E = float(np.log2(np.e))

LANES = 128
SUBLANES = 8
CONV_HALO = 2 * SUBLANES
VMEM_LIMIT = 48 * 1024 * 1024

COL_XBC = 0
COL_SC = COL_XBC + SSM_CONV_DIM
COL_Z = COL_SC + 4 * SC_WIDTH
COL_MERGE = COL_Z + SSM_INNER
PROJ_WIDTH = COL_MERGE + 3 * D_MODEL

PAIRS = ATTN_KV_WIDTH // LANES
SLAB_Q1_Q4 = 0
SLAB_Q16_GATE = SLAB_Q1_Q4 + PAIRS
SLAB_K_V = SLAB_Q16_GATE + PAIRS
N_SLABS = SLAB_K_V + PAIRS
QKV_WIDTH = 2 * N_SLABS * LANES
QCOL_Q1 = 0
QCOL_Q4 = QCOL_Q1 + ATTN_KV_WIDTH
QCOL_Q16 = QCOL_Q4 + ATTN_KV_WIDTH
QCOL_K = QCOL_Q16 + ATTN_KV_WIDTH
QCOL_V = QCOL_K + ATTN_KV_WIDTH
QCOL_GATE = QCOL_V + ATTN_KV_WIDTH
HIGH_HALF = -65536


def _half_silu(hx):
    return hx * jnp.tanh(hx) + hx


def _half_gate2(hx):
    return jnp.tanh(hx) + 1.0


def _pack_pair(a, b):
    ab = lax.bitcast_convert_type(a.astype(BF16).astype(F32), jnp.int32)
    bb = lax.bitcast_convert_type(b.astype(BF16).astype(F32), jnp.int32)
    return ab | lax.shift_right_logical(bb, 16)


def _unpack_high(w):
    return lax.bitcast_convert_type(w & HIGH_HALF, F32)


def _unpack_low(w):
    return lax.bitcast_convert_type(lax.shift_left(w, 16), F32)


def _split_hi_lo(v):
    hi = v.astype(BF16)
    lo = (v - hi.astype(F32)).astype(BF16)
    return jnp.concatenate([hi, lo], axis=1)


def _normed(x_ref, nw_ref):
    x = x_ref[0]
    ms = jnp.mean(x * x, axis=-1, keepdims=True)
    return (x * lax.rsqrt(ms + NORM_EPS) * nw_ref[...]).astype(BF16)


PREP_TILE = 1024


def _wprep_body(base_ref, shift_ref, main_ref, next_ref, scale_ref, o_ref):
    main = main_ref[...]
    both = jnp.concatenate([main, next_ref[...]], axis=0)
    moved = both[SSM_HEADS:SSM_HEADS + PREP_TILE, :]
    src = jnp.where(shift_ref[pl.program_id(1)] > 0, moved, main)
    o_ref[...] = (src.T * scale_ref[...]).astype(BF16)


def _prep_weights(w_in_t, src_cols, col_scale, name):
    depth = w_in_t.shape[0]
    n_tiles = len(src_cols)
    assert all(c % PREP_TILE in (0, SSM_HEADS) for c in src_cols) and SSM_HEADS % SUBLANES == 0
    base = jnp.asarray([c // PREP_TILE for c in src_cols], jnp.int32)
    shift = jnp.asarray([c % PREP_TILE for c in src_cols], jnp.int32)
    per_tile = PREP_TILE // LANES
    grid_spec = pltpu.PrefetchScalarGridSpec(
        num_scalar_prefetch=2,
        grid=(depth, n_tiles),
        in_specs=[
            pl.BlockSpec((None, PREP_TILE, D_MODEL), lambda l, t, b, s: (l, b[t], 0)),
            pl.BlockSpec((None, LANES, D_MODEL), lambda l, t, b, s: (l, (b[t] + 1) * per_tile, 0)),
            pl.BlockSpec((1, PREP_TILE), lambda l, t, b, s: (0, t)),
        ],
        out_specs=pl.BlockSpec((None, D_MODEL, PREP_TILE), lambda l, t, b, s: (l, 0, t)),
    )
    return pl.pallas_call(
        _wprep_body,
        grid_spec=grid_spec,
        out_shape=jax.ShapeDtypeStruct((depth, D_MODEL, n_tiles * PREP_TILE), BF16),
        compiler_params=pltpu.CompilerParams(
            dimension_semantics=("arbitrary", "arbitrary"),
            vmem_limit_bytes=VMEM_LIMIT),
        name=name,
    )(base, shift, w_in_t, w_in_t, jnp.asarray(col_scale, F32).reshape(1, -1))


def _qkvproj_body(x_ref, nw_ref, w_ref, o_ref):
    res = jnp.dot(_normed(x_ref, nw_ref), w_ref[...], preferred_element_type=F32)
    for first, (hi_col, lo_col) in ((SLAB_Q1_Q4, (QCOL_Q1, QCOL_Q4)),
                                    (SLAB_Q16_GATE, (QCOL_Q16, QCOL_GATE)),
                                    (SLAB_K_V, (QCOL_K, QCOL_V))):
        for p in range(PAIRS):
            hi = res[:, hi_col + p * LANES:hi_col + (p + 1) * LANES]
            lo = res[:, lo_col + p * LANES:lo_col + (p + 1) * LANES]
            o_ref[first + p] = _pack_pair(hi, lo)


def _qkv_proj(x, norm_w, w_qkv, layer, tm=512):
    bsz, seq, _ = x.shape
    tps = seq // tm
    return pl.pallas_call(
        _qkvproj_body,
        grid=(bsz * tps,),
        in_specs=[
            pl.BlockSpec((1, tm, D_MODEL), lambda i: (i // tps, i % tps, 0)),
            pl.BlockSpec((1, D_MODEL), lambda i: (0, 0)),
            pl.BlockSpec((None, D_MODEL, QKV_WIDTH), lambda i: (layer, 0, 0)),
        ],
        out_specs=pl.BlockSpec((N_SLABS, tm, LANES), lambda i: (0, i, 0)),
        out_shape=jax.ShapeDtypeStruct((N_SLABS, bsz * seq, LANES), jnp.int32),
        compiler_params=pltpu.CompilerParams(
            dimension_semantics=("arbitrary",),
            vmem_limit_bytes=VMEM_LIMIT),
        name="qkv_proj",
    )(x, norm_w, w_qkv)


def _inproj_body(x_ref, nw_ref, w_ref, wdt_ref, o_ref, dt_ref, h_scr):
    @pl.when(pl.program_id(1) == 0)
    def _():
        h = _normed(x_ref, nw_ref)
        h_scr[...] = h
        dt_ref[0] = jnp.dot(h, wdt_ref[...], preferred_element_type=F32)

    o_ref[0] = jnp.dot(h_scr[...], w_ref[...], preferred_element_type=F32).astype(BF16)


def _in_proj(x, norm_w, w_perm, w_qkv, layer, tm=1024, tn=2048):
    bsz, seq, _ = x.shape
    tps = seq // tm
    return pl.pallas_call(
        _inproj_body,
        grid=(bsz * tps, PROJ_WIDTH // tn),
        in_specs=[
            pl.BlockSpec((1, tm, D_MODEL), lambda i, j: (i // tps, i % tps, 0)),
            pl.BlockSpec((1, D_MODEL), lambda i, j: (0, 0)),
            pl.BlockSpec((None, D_MODEL, tn), lambda i, j: (layer, 0, j)),
            pl.BlockSpec((None, D_MODEL, LANES), lambda i, j: (layer, 0, QKV_WIDTH // LANES)),
        ],
        out_specs=[
            pl.BlockSpec((1, tm, tn), lambda i, j: (i // tps, i % tps, j)),
            pl.BlockSpec((1, tm, LANES), lambda i, j: (i // tps, i % tps, 0)),
        ],
        out_shape=[
            jax.ShapeDtypeStruct((bsz, seq, PROJ_WIDTH), BF16),
            jax.ShapeDtypeStruct((bsz, seq, LANES), F32),
        ],
        scratch_shapes=[pltpu.VMEM((tm, D_MODEL), BF16)],
        compiler_params=pltpu.CompilerParams(
            dimension_semantics=("arbitrary", "arbitrary"),
            vmem_limit_bytes=VMEM_LIMIT),
        name="in_proj",
    )(x, norm_w, w_perm, w_qkv)


def _ssd_body(xbc_ref, z_ref, dt_ref, cw_ref, cb_ref, dtb_ref, aneg_ref,
              dskip_ref, nw_ref, expand_ref, tri_ref, shift_ref, o_ref,
              prev_scr, xs_scr, b_scr, c_scr, bd_scr, state_scr):
    q = SSM_CHUNK

    @pl.when(pl.program_id(1) == 0)
    def _():
        prev_scr[...] = jnp.zeros_like(prev_scr)
        state_scr[...] = jnp.zeros_like(state_scr)
        bd_scr[...] = jnp.zeros_like(bd_scr)

    row = lax.broadcasted_iota(jnp.int32, (q, q), 0)
    col = lax.broadcasted_iota(jnp.int32, (q, q), 1)
    causal = row >= col

    for sub in range(SSD_SUB):
        rows = slice(sub * q, (sub + 1) * q)
        _ssd_chunk(
            xbc_ref.at[0, rows, :],
            prev_scr if sub == 0 else xbc_ref.at[0, slice((sub - 1) * q, sub * q), :],
            z_ref.at[0, rows, :], dt_ref.at[0, rows, :], cw_ref, cb_ref, dtb_ref, aneg_ref,
            dskip_ref, nw_ref, expand_ref, tri_ref, shift_ref, o_ref.at[0, rows, :],
            xs_scr.at[rows, :], b_scr.at[rows, :], c_scr.at[rows, :], bd_scr.at[sub],
            state_scr, causal)
    prev_scr[...] = xbc_ref[0, (SSD_SUB - 1) * q:SSD_SUB * q, :]


def _ssd_chunk(xbc_ref, prev_ref, z_ref, dt_ref, cw_ref, cb_ref, dtb_ref, aneg_ref,
               dskip_ref, nw_ref, expand_ref, tri_ref, shift_ref, o_ref,
               xs_scr, b_scr, c_scr, bd_scr, state_scr, causal):
    q = SSM_CHUNK

    piece = 512
    for p in range(SSM_CONV_DIM // piece):
        cols = slice(p * piece, (p + 1) * piece)
        win = jnp.concatenate([prev_ref[q - CONV_HALO:q, cols], xbc_ref[:, cols]], axis=0)
        taps = jnp.concatenate([win * cw_ref[k:k + 1, cols] for k in range(SSM_CONV)], axis=0)
        acc = jnp.dot(shift_ref[...], taps, preferred_element_type=F32) + cb_ref[:, cols]
        act = _half_silu(acc)
        if p < 2:
            xs_scr[:, cols] = act
        elif p == 2:
            b_scr[...] = act.astype(BF16)
        else:
            c_scr[...] = act.astype(BF16)

    dt_in = dt_ref[...] + dtb_ref[...]
    dt = jnp.maximum(dt_in, 0.0) + jnp.log(1.0 + jnp.exp(-jnp.abs(dt_in)))
    a = dt * aneg_ref[...]
    a_hi = a.astype(BF16)
    r1 = a - a_hi.astype(F32)
    a_mid = r1.astype(BF16)
    a_lo = (r1 - a_mid.astype(F32)).astype(BF16)
    cs = jnp.dot(tri_ref[...], jnp.concatenate([a_hi, a_mid, a_lo], axis=0),
                 preferred_element_type=F32)
    cs2 = cs * LOG2E
    cs2_t = cs2.T
    ecs = jnp.exp(cs)
    dec = jnp.exp(cs[q - 1:q, :] - cs)
    stacked = jnp.concatenate(
        [_split_hi_lo(dt), _split_hi_lo(ecs), _split_hi_lo(dec)], axis=0)
    expanded = jnp.dot(stacked, expand_ref[...], preferred_element_type=F32)
    dt_x = expanded[0:q]
    ecs_x = expanded[q:2 * q]
    dec_x = expanded[2 * q:3 * q]

    xs = xs_scr[...]
    xdt = xs * dt_x
    xdt_b = xdt.astype(BF16)
    xdec_b = (xdt * dec_x).astype(BF16)

    gw = SSM_HEADS_PER_GROUP * SSM_HEAD_DIM
    for g in range(SSM_GROUPS):
        gcols = slice(g * gw, (g + 1) * gw)
        ncols = slice(g * SSM_STATE, (g + 1) * SSM_STATE)
        bg = b_scr[:, ncols]
        cg = c_scr[:, ncols]
        cb = lax.dot_general(cg, bg, (((1,), (1,)), ((), ())),
                             preferred_element_type=F32)
        st = state_scr[:, gcols]
        y_off = jnp.dot(cg, st.astype(BF16),
                        preferred_element_type=F32) * ecs_x[:, gcols]
        m_parts = []
        for j in range(SSM_HEADS_PER_GROUP):
            h = g * SSM_HEADS_PER_GROUP + j
            seg = cs2[:, h:h + 1] - cs2_t[h:h + 1, :]
            lmat = jnp.exp2(jnp.where(causal, seg, -jnp.inf))
            m_parts.append((cb * lmat).astype(BF16))
            bd_scr[g, j * q:(j + 1) * q, j * SSM_HEAD_DIM:(j + 1) * SSM_HEAD_DIM] = (
                xdt_b[:, h * SSM_HEAD_DIM:(h + 1) * SSM_HEAD_DIM])
        y_g = jnp.dot(jnp.concatenate(m_parts, axis=1), bd_scr[g],
                      preferred_element_type=F32) + y_off
        upd = lax.dot_general(bg, xdec_b[:, gcols], (((0,), (0,)), ((), ())),
                              preferred_element_type=F32)
        state_scr[:, gcols] = st * ecs_x[q - 1:q, gcols] + upd

        y_g = y_g + xs[:, gcols] * dskip_ref[:, gcols]
        yg = y_g * _half_silu(z_ref[:, gcols].astype(F32))
        ms = jnp.mean(yg * yg, axis=-1, keepdims=True)
        yg = yg * lax.rsqrt(ms + NORM_EPS) * nw_ref[:, gcols]
        o_ref[:, gcols] = yg.astype(BF16)


def _shift_matrix():
    q = SSM_CHUNK
    win = q + CONV_HALO
    m = np.zeros((q, SSM_CONV * win), np.float32)
    for k in range(SSM_CONV):
        for t in range(q):
            m[t, k * win + CONV_HALO + t - (SSM_CONV - 1) + k] = 1.0
    return jnp.asarray(m, BF16)


def _cumsum_matrix():
    tri = np.tril(np.ones((SSM_CHUNK, SSM_CHUNK), np.float32))
    return jnp.asarray(np.concatenate([tri, tri, tri], axis=1), BF16)


def _ssd(proj3, dt3, cw, cb, dtb, aneg, dskip_x, nw, expand_mat, tri, shift):
    bsz, s, _ = proj3.shape
    q = SSM_CHUNK
    rows = SSD_SUB * q
    const = lambda b, c: (0, 0)
    return pl.pallas_call(
        _ssd_body,
        grid=(bsz, s // rows),
        in_specs=[
            pl.BlockSpec((1, rows, SSM_CONV_DIM), lambda b, c: (b, c, COL_XBC // SSM_CONV_DIM)),
            pl.BlockSpec((1, rows, SSM_INNER), lambda b, c: (b, c, COL_Z // SSM_INNER)),
            pl.BlockSpec((1, rows, LANES), lambda b, c: (b, c, 0)),
            pl.BlockSpec((SSM_CONV, SSM_CONV_DIM), const),
            pl.BlockSpec((1, SSM_CONV_DIM), const),
            pl.BlockSpec((1, LANES), const),
            pl.BlockSpec((1, LANES), const),
            pl.BlockSpec((1, SSM_INNER), const),
            pl.BlockSpec((1, SSM_INNER), const),
            pl.BlockSpec((2 * LANES, SSM_INNER), const),
            pl.BlockSpec((q, 3 * q), const),
            pl.BlockSpec((q, SSM_CONV * (q + CONV_HALO)), const),
        ],
        out_specs=pl.BlockSpec((1, rows, SSM_INNER), lambda b, c: (b, c, 0)),
        out_shape=jax.ShapeDtypeStruct((bsz, s, SSM_INNER), BF16),
        scratch_shapes=[
            pltpu.VMEM((q, SSM_CONV_DIM), BF16),
            pltpu.VMEM((rows, SSM_INNER), F32),
            pltpu.VMEM((rows, SSM_GROUPS * SSM_STATE), BF16),
            pltpu.VMEM((rows, SSM_GROUPS * SSM_STATE), BF16),
            pltpu.VMEM((SSD_SUB, SSM_GROUPS, SSM_HEADS_PER_GROUP * q,
                        SSM_HEADS_PER_GROUP * SSM_HEAD_DIM), BF16),
            pltpu.VMEM((SSM_STATE, SSM_INNER), F32),
        ],
        compiler_params=pltpu.CompilerParams(
            dimension_semantics=("arbitrary", "arbitrary"),
            vmem_limit_bytes=VMEM_LIMIT),
        name="ssd",
    )(proj3, proj3, dt3, cw, cb, dtb, aneg, dskip_x, nw, expand_mat, tri, shift)


def _attn_block(q, k, v, bias):
    nq, nk = bias.shape
    low = lax.broadcasted_iota(jnp.int32, (nq, LANES), 1) < ATTN_HEAD_DIM
    q2 = jnp.concatenate([jnp.where(low, q, 0.0), jnp.where(low, 0.0, q)],
                         axis=0).astype(BF16)
    s = lax.dot_general(q2, k.astype(BF16), (((1,), (1,)), ((), ())),
                        preferred_element_type=F32)
    s = s + jnp.concatenate([bias, bias], axis=0)
    mx = jnp.max(s, axis=-1, keepdims=True)
    p = jnp.exp2(s - mx).astype(BF16)
    vext = jnp.concatenate([v.astype(BF16), jnp.ones((nk, LANES), BF16)], axis=1)
    r = jnp.dot(p, vext, preferred_element_type=F32)
    acc = jnp.where(low, r[0:nq, 0:LANES], r[nq:, 0:LANES])
    den = jnp.where(low, r[0:nq, LANES:], r[nq:, LANES:])
    mxp = jnp.where(low, mx[0:nq], mx[nq:])
    return acc / den, mxp + jnp.log2(den)


def _attn_body(qa_ref, qb_ref, kv_ref, out_ref,
               o4_scr, l4_scr, o16_scr, l16_scr, bias_scr, wbias_scr):
    blk = ATTN_BLOCK
    seq = kv_ref.shape[1]
    (w1, d1), (w4, d4), (w16, d16) = ATTN_PATTERNS
    n_back = w1 // d1
    assert n_back == w4 // d4 == w16 // d16 == blk

    qi = lax.broadcasted_iota(jnp.int32, (blk, 2 * blk), 0) + blk
    kj = lax.broadcasted_iota(jnp.int32, (blk, 2 * blk), 1)
    diff = qi - kj
    band = (diff >= 0) & (diff <= n_back)
    bias_scr[0] = jnp.where(band & (kj >= blk), 0.0, -jnp.inf)
    bias_scr[1] = jnp.where(band, 0.0, -jnp.inf)
    wdiff = (lax.broadcasted_iota(jnp.int32, (2 * blk, 2 * blk), 0)
             - lax.broadcasted_iota(jnp.int32, (2 * blk, 2 * blk), 1))
    wbias_scr[...] = jnp.where((wdiff >= 0) & (wdiff <= n_back), 0.0, -jnp.inf)

    def whole_phase(q_ref, unpack_q, o_scr, l_scr, dil):
        length = seq // dil
        assert length == 2 * blk

        def body(r, carry):
            rows = pl.ds(r, length, stride=dil)
            kv = kv_ref[0, rows, :]
            o, lse = _attn_block(unpack_q(q_ref[0, rows, :]), _unpack_high(kv), _unpack_low(kv),
                                 wbias_scr[...])
            o_scr[rows, :] = o
            l_scr[rows, :] = lse
            return carry

        lax.fori_loop(0, dil, body, 0, unroll=ATTN_UNROLL // 2)

    def strided_phase(q_ref, unpack_q, o_scr, l_scr, dil):
        span = blk * dil

        def body(idx, carry):
            r = idx % dil
            n = idx // dil
            cur = r + span * n
            prv = jnp.maximum(cur - span, r)

            def ld(ref, start):
                return ref[0, pl.ds(start, blk, stride=dil), :]

            kv = jnp.concatenate([ld(kv_ref, prv), ld(kv_ref, cur)], axis=0)
            o, lse = _attn_block(unpack_q(ld(q_ref, cur)), _unpack_high(kv), _unpack_low(kv),
                                 bias_scr[jnp.minimum(n, 1)])
            o_scr[pl.ds(cur, blk, stride=dil), :] = o
            l_scr[pl.ds(cur, blk, stride=dil), :] = lse
            return carry

        lax.fori_loop(0, seq // blk, body, 0, unroll=ATTN_UNROLL)

    whole_phase(qb_ref, _unpack_high, o16_scr, l16_scr, d16)
    strided_phase(qa_ref, _unpack_low, o4_scr, l4_scr, d4)

    def body(n, carry):
        cur = pl.multiple_of(n * blk, blk)
        prv = pl.multiple_of(jnp.maximum(cur - blk, 0), blk)
        rows = pl.ds(cur, blk)
        prows = pl.ds(prv, blk)
        kv = jnp.concatenate([kv_ref[0, prows, :], kv_ref[0, rows, :]], axis=0)
        o1, l1 = _attn_block(_unpack_high(qa_ref[0, rows, :]), _unpack_high(kv), _unpack_low(kv),
                             bias_scr[jnp.minimum(n, 1)])
        l4, l16 = l4_scr[rows, :], l16_scr[rows, :]
        lm = jnp.maximum(jnp.maximum(l1, l4), l16)
        e1, e4, e16 = jnp.exp2(l1 - lm), jnp.exp2(l4 - lm), jnp.exp2(l16 - lm)
        y = (e1 * o1 + e4 * o4_scr[rows, :] + e16 * o16_scr[rows, :]) / (e1 + e4 + e16)
        gate = _half_silu(_unpack_low(qb_ref[0, rows, :]))
        out_ref[0, rows, :] = (y * gate).astype(BF16)
        return carry

    lax.fori_loop(0, seq // blk, body, 0, unroll=ATTN_UNROLL)


def _attention(slabs, bsz, seq):
    slab = lambda s0: (lambda b, p: (s0 + p, b, 0))
    spec = lambda s0: pl.BlockSpec((1, seq, LANES), slab(s0))
    return pl.pallas_call(
        _attn_body,
        grid=(bsz, PAIRS),
        in_specs=[spec(SLAB_Q1_Q4), spec(SLAB_Q16_GATE), spec(SLAB_K_V)],
        out_specs=pl.BlockSpec((1, seq, LANES), lambda b, p: (b, 0, p)),
        out_shape=jax.ShapeDtypeStruct((bsz, seq, ATTN_KV_WIDTH), BF16),
        scratch_shapes=[pltpu.VMEM((seq, LANES), F32) for _ in range(4)]
        + [pltpu.VMEM((2, ATTN_BLOCK, 2 * ATTN_BLOCK), F32),
           pltpu.VMEM((2 * ATTN_BLOCK, 2 * ATTN_BLOCK), F32)],
        compiler_params=pltpu.CompilerParams(
            dimension_semantics=("arbitrary", "arbitrary"),
            vmem_limit_bytes=VMEM_LIMIT),
        name="dilated_attention",
    )(slabs, slabs, slabs)


def _merge_body(x_ref, sc_ref, m0_ref, m1_ref, m2_ref, yssm_ref, yattn_ref,
                scw_ref, pssm_ref, pattn_ref, psc_ref, wout_ref, nw_ref,
                out_ref, ext_scr, *, tiles_per_seq):
    tm = x_ref.shape[1]
    halo = SUBLANES
    first = (pl.program_id(0) % tiles_per_seq) == 0

    @pl.when(first)
    def _():
        ext_scr[0:halo, :] = jnp.zeros((halo, SC_WIDTH), F32)

    @pl.when(jnp.logical_not(first))
    def _():
        ext_scr[0:halo, :] = ext_scr[tm:tm + halo, :]

    u = sc_ref[0, :, 0:SC_WIDTH].astype(F32)
    b_sc = sc_ref[0, :, SC_WIDTH:2 * SC_WIDTH].astype(F32)
    c_sc = sc_ref[0, :, 2 * SC_WIDTH:3 * SC_WIDTH].astype(F32)
    g_sc = sc_ref[0, :, 3 * SC_WIDTH:4 * SC_WIDTH].astype(F32)
    ext_scr[halo:halo + tm, :] = c_sc * u
    conv = jnp.zeros((tm, SC_WIDTH), F32)
    for k in range(SC_CONV):
        lo = halo - (SC_CONV - 1) + k
        conv = conv + scw_ref[k:k + 1, :] * ext_scr[lo:lo + tm, :]
    y_sc = (b_sc * conv * _half_silu(g_sc)).astype(BF16)

    merged = (
        _half_gate2(m0_ref[0].astype(F32))
        * jnp.dot(yssm_ref[0], pssm_ref[...], preferred_element_type=F32)
        + _half_gate2(m1_ref[0].astype(F32))
        * jnp.dot(yattn_ref[0], pattn_ref[...], preferred_element_type=F32)
        + _half_gate2(m2_ref[0].astype(F32))
        * jnp.dot(y_sc, psc_ref[...], preferred_element_type=F32))
    out = jnp.dot(merged.astype(BF16), wout_ref[...], preferred_element_type=F32)
    ms = jnp.mean(out * out, axis=-1, keepdims=True)
    out_ref[0] = x_ref[0] + out * lax.rsqrt(ms + NORM_EPS) * nw_ref[...]


def _merge(x, proj, y_ssm, y_attn, scw, w_stack, nw, tm=512):
    bsz, seq, _ = x.shape
    tps = seq // tm
    row = lambda c: (lambda i: (i // tps, i % tps, c))
    const = lambda i: (0, 0)
    wrows = lambda r: (lambda i: (r, 0))
    wa = ATTN_KV_WIDTH
    assert SSM_INNER == D_MODEL == 2 * wa and wa == SC_WIDTH
    return pl.pallas_call(
        functools.partial(_merge_body, tiles_per_seq=tps),
        grid=(bsz * tps,),
        in_specs=[
            pl.BlockSpec((1, tm, D_MODEL), row(0)),
            pl.BlockSpec((1, tm, 4 * SC_WIDTH), row(COL_SC // (4 * SC_WIDTH))),
            pl.BlockSpec((1, tm, D_MODEL), row(COL_MERGE // D_MODEL)),
            pl.BlockSpec((1, tm, D_MODEL), row(COL_MERGE // D_MODEL + 1)),
            pl.BlockSpec((1, tm, D_MODEL), row(COL_MERGE // D_MODEL + 2)),
            pl.BlockSpec((1, tm, SSM_INNER), row(0)),
            pl.BlockSpec((1, tm, wa), row(0)),
            pl.BlockSpec((SC_CONV, SC_WIDTH), const),
            pl.BlockSpec((SSM_INNER, D_MODEL), wrows(0)),
            pl.BlockSpec((wa, D_MODEL), wrows(2)),
            pl.BlockSpec((SC_WIDTH, D_MODEL), wrows(3)),
            pl.BlockSpec((D_MODEL, D_MODEL), wrows(2)),
            pl.BlockSpec((1, D_MODEL), const),
        ],
        out_specs=pl.BlockSpec((1, tm, D_MODEL), row(0)),
        out_shape=jax.ShapeDtypeStruct((bsz, seq, D_MODEL), F32),
        scratch_shapes=[pltpu.VMEM((tm + 2 * SUBLANES, SC_WIDTH), F32)],
        compiler_params=pltpu.CompilerParams(
            dimension_semantics=("arbitrary",),
            vmem_limit_bytes=VMEM_LIMIT),
        name="merge_out",
    )(x, proj, proj, proj, proj, y_ssm, y_attn,
      scw, w_stack, w_stack, w_stack, w_stack, nw)


def _expand_matrix(n_heads, width):
    m = np.zeros((2 * LANES, n_heads * width), np.float32)
    for h in range(n_heads):
        m[h, h * width:(h + 1) * width] = 1.0
        m[LANES + h, h * width:(h + 1) * width] = 1.0
    return jnp.asarray(m, BF16)


def _pad_lanes(v):
    return jnp.pad(v.astype(F32), (0, LANES - v.shape[0]))[None, :]


def _prepare_w_in(w_in):
    o_z, o_xbc, o_dt, o_q, o_k, o_v, o_gat, o_sc, o_mg, o_end = np.cumsum(
        [0, SSM_INNER, SSM_CONV_DIM, SSM_HEADS, 3 * ATTN_KV_WIDTH, ATTN_KV_WIDTH,
         ATTN_KV_WIDTH, ATTN_KV_WIDTH, 4 * SC_WIDTH, 3 * D_MODEL])
    tiles = lambda start, width: [int(start) + c for c in range(0, width, PREP_TILE)]
    w_in = jnp.swapaxes(w_in, 1, 2)
    qkv_scale = np.ones((QKV_WIDTH,), np.float32)
    qkv_scale[QCOL_Q1:QCOL_K] = ATTN_HEAD_DIM ** -0.5 * LOG2E
    qkv_scale[QCOL_GATE:] = 0.5
    dt_scale = np.zeros((PREP_TILE,), np.float32)
    dt_scale[:SSM_HEADS] = 1.0
    w_qkv = _prep_weights(w_in, tiles(o_q, QKV_WIDTH) + [int(o_dt)],
                          np.concatenate([qkv_scale, dt_scale]), "prep_w_qkv")
    perm_scale = np.ones((PROJ_WIDTH,), np.float32)
    perm_scale[COL_SC + 3 * SC_WIDTH:] = 0.5
    perm_cols = (tiles(o_xbc, SSM_CONV_DIM) + tiles(o_sc, 4 * SC_WIDTH)
                 + tiles(o_z, SSM_INNER) + tiles(o_mg, 3 * D_MODEL))
    w_perm = _prep_weights(w_in, perm_cols, perm_scale, "prep_w_proj")
    return w_qkv, w_perm


def kernel(x, norm_pre, norm_post, w_in, ssm_conv_w, ssm_conv_b, dt_bias, a_log,
           d_skip, ssm_norm, sc_conv_w, p_ssm, p_attn, p_sc, w_out):
    bsz, seq, _ = x.shape
    expand_ssm = _expand_matrix(SSM_HEADS, SSM_HEAD_DIM)
    cumsum_mat = _cumsum_matrix()
    shift = _shift_matrix()
    w_qkv, w_perm = _prepare_w_in(w_in)
    for i in range(norm_pre.shape[0]):
        nw_pre = norm_pre[i].reshape(1, D_MODEL)
        conv_w = (0.5 * ssm_conv_w[i]).astype(BF16)
        conv_b = (0.5 * ssm_conv_b[i]).reshape(1, SSM_CONV_DIM)
        dskip_x = jnp.repeat(d_skip[i].astype(F32), SSM_HEAD_DIM).reshape(1, SSM_INNER)

        slabs = _qkv_proj(x, nw_pre, w_qkv, i)
        proj, dt_raw = _in_proj(x, nw_pre, w_perm, w_qkv, i)
        y_ssm = _ssd(proj, dt_raw, conv_w, conv_b, _pad_lanes(dt_bias[i]),
                     _pad_lanes(-jnp.exp(a_log[i].astype(F32))), dskip_x,
                     ssm_norm[i].reshape(1, SSM_INNER), expand_ssm, cumsum_mat, shift)
        y_attn = _attention(slabs, bsz, seq)
        w_stack = jnp.concatenate(
            [0.5 * p_ssm[i], 0.5 * p_attn[i], 0.5 * p_sc[i], w_out[i]], axis=0).astype(BF16)
        x = _merge(x, proj, y_ssm, y_attn, sc_conv_w[i], w_stack,
                   norm_post[i].reshape(1, D_MODEL))
    return x
```
